```python
import math
import jax
import jax.numpy as jnp
from jax import lax
import numpy as np

D_MODEL = 2048
BATCH = 32
SEQ = 256
DEPTH = 4
DEC_BATCH = 4
DEC_SEQ = 2048
PAST_LEN = 256

GRID_W = 64
N_MIXERS = 2
N_MLA = (DEPTH + 1) // 2
N_DIFF = DEPTH // 2
MLA_HEADS = 16
MLA_Q_RANK = 512
MLA_KV_RANK = 512
MLA_NOPE = 128
MLA_ROPE = 64
MLA_V = 128
DIFF_HEADS = 16
DIFF_DK = 64
DIFF_DV = 2 * DIFF_DK
N_EXPERTS = 16
EXPERT_DIM = 2048
CAP_FACTOR = 2
ROPE_BASE = 10000.0
Q_BLOCK = 128
NORM_EPS = 1e-6
DEEPNORM_ALPHA = (2.0 * DEPTH) ** 0.25
DEEPNORM_BETA = (8.0 * DEPTH) ** -0.25

kernel_name = 'hybrid_mla_diffattn_ec_moe_diffusion_step'


def layer_norm(x, g, b):
    xf = x.astype(jnp.float32)
    mu = jnp.mean(xf, axis=-1, keepdims=True)
    var = jnp.mean(jnp.square(xf - mu), axis=-1, keepdims=True)
    y = (xf - mu) * lax.rsqrt(var + NORM_EPS)
    return (y * g.astype(jnp.float32) + b.astype(jnp.float32)).astype(x.dtype)


def rms_norm(x, g):
    xf = x.astype(jnp.float32)
    y = xf * lax.rsqrt(jnp.mean(jnp.square(xf), axis=-1, keepdims=True) + NORM_EPS)
    return (y * g.astype(jnp.float32)).astype(x.dtype)


def axial_rope_tables(n_tok, rot_dim, dtype):
    rows = n_tok // GRID_W
    row = jnp.repeat(jnp.arange(rows, dtype=jnp.float32), GRID_W)
    col = jnp.tile(jnp.arange(GRID_W, dtype=jnp.float32), rows)
    half = rot_dim // 2
    inv_freq = ROPE_BASE ** (-jnp.arange(0, half, 2, dtype=jnp.float32) / half)
    ang_r = row[:, None] * inv_freq
    ang_c = col[:, None] * inv_freq
    ang = jnp.concatenate([ang_r, ang_r, ang_c, ang_c], axis=-1)
    return jnp.cos(ang).astype(dtype), jnp.sin(ang).astype(dtype)


def rotate_axial(x):
    a, b, c, e = jnp.split(x, 4, axis=-1)
    return jnp.concatenate([-b, a, -e, c], axis=-1)


def apply_rope(x, cos, sin):
    return x * cos + rotate_axial(x) * sin


def sweep_query_blocks(attend, *qs):
    B, T = qs[0].shape[:2]
    nb = T // Q_BLOCK
    blocks = tuple(jnp.moveaxis(q.reshape(B, nb, Q_BLOCK, *q.shape[2:]), 1, 0) for q in qs)
    out = lax.map(lambda blk: attend(*blk), blocks)
    return jnp.moveaxis(out, 0, 1).reshape(B, T, *out.shape[3:])


def ada_mod(cvec, w, b):
    mod = jax.nn.silu(cvec) @ w + b
    return jnp.split(mod[:, None, :], 6, axis=-1)


def mla_kv_latent(h, w_dkv, kv_norm):
    ckv, kpe = jnp.split(h @ w_dkv, [MLA_KV_RANK], axis=-1)
    return rms_norm(ckv, kv_norm), kpe


def mla_queries(h, w_dq, q_norm, w_uq):
    B, T, _ = h.shape
    q = (rms_norm(h @ w_dq, q_norm) @ w_uq).reshape(B, T, MLA_HEADS, MLA_NOPE + MLA_ROPE)
    return q[..., :MLA_NOPE], q[..., MLA_NOPE:]


def mla_attend(q_nope, q_pe, ckv, kpe, w_ukv, w_o):
    B, Tk, _ = ckv.shape
    kv = (ckv @ w_ukv).reshape(B, Tk, MLA_HEADS, MLA_NOPE + MLA_V)
    k_nope, v = kv[..., :MLA_NOPE], kv[..., MLA_NOPE:]
    scale = (MLA_NOPE + MLA_ROPE) ** -0.5

    def attend(qn, qp):
        s = jnp.einsum('bqhd,bkhd->bhqk', qn, k_nope) + jnp.einsum('bqhr,bkr->bhqk', qp, kpe)
        p = jax.nn.softmax(s.astype(jnp.float32) * scale, axis=-1).astype(v.dtype)
        return jnp.einsum('bhqk,bkhd->bqhd', p, v)

    o = sweep_query_blocks(attend, q_nope, q_pe)
    return o.reshape(o.shape[0], o.shape[1], MLA_HEADS * MLA_V) @ w_o


def mla_context(h, w_dq, q_norm, w_uq, w_dkv, kv_norm, w_ukv, w_o):
    ckv, kpe = mla_kv_latent(h, w_dkv, kv_norm)
    q_nope, q_pe = mla_queries(h, w_dq, q_norm, w_uq)
    return mla_attend(q_nope, q_pe, ckv, kpe, w_ukv, w_o), ckv, kpe


def mla_latent(h, cache_ckv, cache_kpe, w_dq, q_norm, w_uq, w_dkv, kv_norm, w_ukv, w_o):
    cos, sin = axial_rope_tables(h.shape[1], MLA_ROPE, h.dtype)
    ckv, kpe = mla_kv_latent(h, w_dkv, kv_norm)
    kpe = apply_rope(kpe, cos, sin)
    q_nope, q_pe = mla_queries(h, w_dq, q_norm, w_uq)
    q_pe = apply_rope(q_pe, cos[:, None, :], sin[:, None, :])
    ckv_all = jnp.concatenate([cache_ckv.astype(h.dtype), ckv], axis=1)
    kpe_all = jnp.concatenate([cache_kpe.astype(h.dtype), kpe], axis=1)
    return mla_attend(q_nope, q_pe, ckv_all, kpe_all, w_ukv, w_o)


def diff_lambda(lq1, lk1, lq2, lk2, lam_init):
    f = lambda a: a.astype(jnp.float32)
    return jnp.exp(jnp.sum(f(lq1) * f(lk1))) - jnp.exp(jnp.sum(f(lq2) * f(lk2))) + lam_init


def diff_project(h, w_qkv):
    B, T, _ = h.shape
    nq = DIFF_HEADS * 2 * DIFF_DK
    q, k, v = jnp.split(h @ w_qkv, [nq, 2 * nq], axis=-1)
    return (q.reshape(B, T, DIFF_HEADS, 2, DIFF_DK),
            k.reshape(B, T, DIFF_HEADS, 2, DIFF_DK),
            v.reshape(B, T, DIFF_HEADS, DIFF_DV))


def diff_attend(q, k, v, lam, lam_init, subln, w_o):
    B, T = q.shape[:2]
    scale = DIFF_DK ** -0.5

    def attend(qb):
        s = jnp.einsum('bqhmd,bkhmd->bhmqk', qb, k).astype(jnp.float32) * scale
        p = jax.nn.softmax(s, axis=-1)
        a = (p[:, :, 0] - lam * p[:, :, 1]).astype(v.dtype)
        return jnp.einsum('bhqk,bkhd->bqhd', a, v)

    o = sweep_query_blocks(attend, q)
    o = rms_norm(o, subln) * (1.0 - lam_init)
    return o.reshape(B, T, DIFF_HEADS * DIFF_DV) @ w_o


def diff_context(h, lam, lam_init, w_qkv, subln, w_o):
    q, k, v = diff_project(h, w_qkv)
    return diff_attend(q, k, v, lam, lam_init, subln, w_o), k, v


def diff_latent(h, cache_k, cache_v, lam, lam_init, w_qkv, subln, w_o):
    cos, sin = axial_rope_tables(h.shape[1], DIFF_DK, h.dtype)
    cs, sn = cos[:, None, None, :], sin[:, None, None, :]
    q, k, v = diff_project(h, w_qkv)
    q = apply_rope(q, cs, sn)
    k = apply_rope(k, cs, sn)
    k_all = jnp.concatenate([cache_k.astype(h.dtype), k], axis=1)
    v_all = jnp.concatenate([cache_v.astype(h.dtype), v], axis=1)
    return diff_attend(q, k_all, v_all, lam, lam_init, subln, w_o)


def expert_choice_ffn(h, w_router, w_gate, w_up, w_down):
    B, T, D = h.shape
    cap = CAP_FACTOR * T // N_EXPERTS
    aff = jax.nn.softmax(jnp.einsum('btd,de->bte', h, w_router).astype(jnp.float32), axis=-1)
    g, idx = lax.top_k(jnp.swapaxes(aff, 1, 2), cap)
    xs = jax.vmap(lambda hb, ib: hb[ib])(h, idx)
    a = jnp.einsum('becd,edf->becf', xs, w_gate)
    u = jnp.einsum('becd,edf->becf', xs, w_up)
    y = jnp.einsum('becf,efd->becd', jax.nn.silu(a) * u, w_down)
    y = y * g[..., None].astype(y.dtype)
    return jax.vmap(lambda ib, yb: jnp.zeros((T, D), yb.dtype).at[ib].add(yb))(idx, y)


def setup_inputs(seed: int = 0) -> dict:
    key = jax.random.key(seed)
    ks = iter(jax.random.split(key, 48))

    def nrm(shape, scale=1.0):
        return jax.random.normal(next(ks), shape, jnp.float32) * scale

    def gain(shape):
        return 1.0 + nrm(shape, 0.05)

    D = D_MODEL
    beta = DEEPNORM_BETA
    Hm, Hd = MLA_HEADS, DIFF_HEADS
    inp = {}
    inp['x_prompt'] = nrm((BATCH, SEQ, D))
    inp['x_sample'] = nrm((DEC_BATCH, DEC_SEQ, D))
    inp['cache_mla_ckv'] = nrm((DEC_BATCH, N_MLA, PAST_LEN, MLA_KV_RANK))
    inp['cache_mla_kpe'] = nrm((DEC_BATCH, N_MLA, PAST_LEN, MLA_ROPE))
    inp['cache_diff_k'] = nrm((DEC_BATCH, N_DIFF, PAST_LEN, Hd, 2, DIFF_DK))
    inp['cache_diff_v'] = nrm((DEC_BATCH, N_DIFF, PAST_LEN, Hd, DIFF_DV))
    inp['c'] = nrm((DEC_BATCH, D))
    inp['c_ctx'] = nrm((D,))
    inp['ada_w'] = nrm((DEPTH, D, 6 * D), 0.5 * D ** -0.5)
    inp['ada_b'] = nrm((DEPTH, 6 * D), 0.02)
    inp['ln1_g'] = gain((DEPTH, D))
    inp['ln1_b'] = nrm((DEPTH, D), 0.02)
    inp['ln2_g'] = gain((DEPTH, D))
    inp['ln2_b'] = nrm((DEPTH, D), 0.02)
    inp['mla_w_dq'] = nrm((N_MLA, D, MLA_Q_RANK), D ** -0.5)
    inp['mla_q_norm'] = gain((N_MLA, MLA_Q_RANK))
    inp['mla_w_uq'] = nrm((N_MLA, MLA_Q_RANK, Hm * (MLA_NOPE + MLA_ROPE)), MLA_Q_RANK ** -0.5)
    inp['mla_w_dkv'] = nrm((N_MLA, D, MLA_KV_RANK + MLA_ROPE), D ** -0.5)
    inp['mla_kv_norm'] = gain((N_MLA, MLA_KV_RANK))
    w_uk = nrm((N_MLA, MLA_KV_RANK, Hm, MLA_NOPE), MLA_KV_RANK ** -0.5)
    w_uv = nrm((N_MLA, MLA_KV_RANK, Hm, MLA_V), beta * MLA_KV_RANK ** -0.5)
    inp['mla_w_ukv'] = jnp.concatenate([w_uk, w_uv], axis=-1).reshape(
        N_MLA, MLA_KV_RANK, Hm * (MLA_NOPE + MLA_V))
    inp['mla_w_o'] = nrm((N_MLA, Hm * MLA_V, D), beta * (Hm * MLA_V) ** -0.5)
    w_qk = nrm((N_DIFF, D, 2 * Hd * 2 * DIFF_DK), D ** -0.5)
    w_v = nrm((N_DIFF, D, Hd * DIFF_DV), beta * D ** -0.5)
    inp['diff_w_qkv'] = jnp.concatenate([w_qk, w_v], axis=-1)
    inp['diff_lambda_q1'] = nrm((N_DIFF, DIFF_DK), 0.1)
    inp['diff_lambda_k1'] = nrm((N_DIFF, DIFF_DK), 0.1)
    inp['diff_lambda_q2'] = nrm((N_DIFF, DIFF_DK), 0.1)
    inp['diff_lambda_k2'] = nrm((N_DIFF, DIFF_DK), 0.1)
    inp['diff_subln'] = gain((N_DIFF, DIFF_DV))
    inp['diff_w_o'] = nrm((N_DIFF, Hd * DIFF_DV, D), beta * (Hd * DIFF_DV) ** -0.5)
    inp['moe_w_router'] = nrm((DEPTH, D, N_EXPERTS), D ** -0.5)
    inp['moe_w_gate'] = nrm((DEPTH, N_EXPERTS, D, EXPERT_DIM), D ** -0.5)
    inp['moe_w_up'] = nrm((DEPTH, N_EXPERTS, D, EXPERT_DIM), beta * D ** -0.5)
    inp['moe_w_down'] = nrm((DEPTH, N_EXPERTS, EXPERT_DIM, D), beta * EXPERT_DIM ** -0.5)
    return inp


def reference(x_prompt, x_sample, cache_mla_ckv, cache_mla_kpe, cache_diff_k, cache_diff_v,
              c, c_ctx, ada_w, ada_b, ln1_g, ln1_b, ln2_g, ln2_b,
              mla_w_dq, mla_q_norm, mla_w_uq, mla_w_dkv, mla_kv_norm, mla_w_ukv, mla_w_o,
              diff_w_qkv, diff_lambda_q1, diff_lambda_k1, diff_lambda_q2, diff_lambda_k2,
              diff_subln, diff_w_o, moe_w_router, moe_w_gate, moe_w_up, moe_w_down):
    xp, xs = x_prompt, x_sample
    c_prompt = c_ctx[None, :]
    st_ckv, st_kpe, st_dk, st_dv = [], [], [], []
    for l in range(DEPTH):
        mp = ada_mod(c_prompt, ada_w[l], ada_b[l])
        ms = ada_mod(c, ada_w[l], ada_b[l])
        hp = xp * (1 + mp[1]) + mp[0]
        hs = xs * (1 + ms[1]) + ms[0]
        j = l // N_MIXERS
        if l % N_MIXERS == 0:
            w = (mla_w_dq[j], mla_q_norm[j], mla_w_uq[j], mla_w_dkv[j], mla_kv_norm[j],
                 mla_w_ukv[j], mla_w_o[j])
            yp, ckv, kpe = mla_context(hp, *w)
            st_ckv.append(ckv)
            st_kpe.append(kpe)
            ys = mla_latent(hs, cache_mla_ckv[:, j], cache_mla_kpe[:, j], *w)
        else:
            lam_init = 0.8 - 0.6 * math.exp(-0.3 * l)
            lam = diff_lambda(diff_lambda_q1[j], diff_lambda_k1[j],
                              diff_lambda_q2[j], diff_lambda_k2[j], lam_init)
            yp, k, v = diff_context(hp, lam, lam_init, diff_w_qkv[j], diff_subln[j], diff_w_o[j])
            st_dk.append(k)
            st_dv.append(v)
            ys = diff_latent(hs, cache_diff_k[:, j], cache_diff_v[:, j], lam, lam_init,
                             diff_w_qkv[j], diff_subln[j], diff_w_o[j])
        xp = layer_norm(DEEPNORM_ALPHA * xp + mp[2] * yp, ln1_g[l], ln1_b[l])
        xs = layer_norm(DEEPNORM_ALPHA * xs + ms[2] * ys, ln1_g[l], ln1_b[l])
        hp = xp * (1 + mp[4]) + mp[3]
        hs = xs * (1 + ms[4]) + ms[3]
        yp = expert_choice_ffn(hp, moe_w_router[l], moe_w_gate[l], moe_w_up[l], moe_w_down[l])
        ys = expert_choice_ffn(hs, moe_w_router[l], moe_w_gate[l], moe_w_up[l], moe_w_down[l])
        xp = layer_norm(DEEPNORM_ALPHA * xp + mp[5] * yp, ln2_g[l], ln2_b[l])
        xs = layer_norm(DEEPNORM_ALPHA * xs + ms[5] * ys, ln2_g[l], ln2_b[l])
    state_mla_ckv = jnp.stack(st_ckv, axis=1)
    state_mla_kpe = jnp.stack(st_kpe, axis=1)
    state_diff_k = jnp.stack(st_dk, axis=1)
    state_diff_v = jnp.stack(st_dv, axis=1)
    return (xp, xs, state_mla_ckv, state_mla_kpe, state_diff_k, state_diff_v)
```

```python
import functools
import math

import jax
import jax.numpy as jnp
from jax import lax
from jax.experimental import pallas as pl
from jax.experimental.pallas import tpu as pltpu

F32 = jnp.float32
BF16 = jnp.bfloat16

NORM_EPS = 1e-6
ROPE_BASE = 10000.0
GRID_W = 64
CAP_FACTOR = 2
MLA_NOPE, MLA_ROPE, MLA_V = 128, 64, 128
DIFF_DK, DIFF_DV = 64, 128
ROT_DIM = 64
LANE = 128
MOD_ROWS = 8
V7X_VMEM_BYTES = 64 * 2**20
VMEM_HEADROOM_BYTES = 6 * 2**20


def _params(semantics, vmem_bytes):
    limit = min(int(vmem_bytes), V7X_VMEM_BYTES - VMEM_HEADROOM_BYTES)
    return pltpu.CompilerParams(dimension_semantics=semantics, vmem_limit_bytes=limit)


def _tile(n, target, mult=16):
    t = min(n, target)
    while t > mult and (n % t or t % mult):
        t -= mult
    assert n % t == 0 and t % mult == 0, (n, target, mult)
    return t


def _dot(a, b):
    return jnp.dot(a, b, preferred_element_type=F32)


def _dot_nt(a, b):
    return lax.dot_general(a, b, (((1,), (1,)), ((), ())), preferred_element_type=F32)


def _dot_tn(a, b):
    return lax.dot_general(a, b, (((0,), (0,)), ((), ())), preferred_element_type=F32)


def _sigmoid(x):
    return 1.0 / (1.0 + jnp.exp(-x))


def _rms(x):
    return x * lax.rsqrt(jnp.mean(x * x, axis=-1, keepdims=True) + NORM_EPS)


def _layer_norm(z, g, b):
    mu = jnp.mean(z, axis=-1, keepdims=True)
    zc = z - mu
    var = jnp.mean(zc * zc, axis=-1, keepdims=True)
    return zc * lax.rsqrt(var + NORM_EPS) * g + b


def _rope(x, cos, sin):
    n = x.shape[-1]
    nxt = pltpu.roll(x, n - 16, 1)
    prv = pltpu.roll(x, 16, 1)
    lane = lax.broadcasted_iota(jnp.int32, x.shape, 1)
    rot = jnp.where((lane % 32) < 16, -nxt, prv)
    return x * cos + rot * sin


def _lane_tile(t, reps):
    return t if reps == 1 else jnp.concatenate([t] * reps, axis=-1)


def _mod_row(i, tm, mp, ts):
    r0 = i * tm
    return jnp.where(r0 < mp, 0, 1 + (r0 - mp) // ts)


def _mod_spec(which, d, row_fn):
    return pl.BlockSpec((None, 1, d), lambda *ids: (row_fn(*ids), 0, which))


def _row_vec(v):
    return v.reshape(1, -1)


def _mod_body(c_ref, w_ref, b_ref, o_ref):
    c = c_ref[...]
    s = (c * _sigmoid(c)).astype(BF16)
    o_ref[...] = _dot(s, w_ref[...].astype(BF16)) + b_ref[...]


def _mod_call(cv, ada_w, ada_b):
    nl, d, n6 = ada_w.shape
    tn = _tile(n6, 1024, LANE)
    return pl.pallas_call(
        _mod_body,
        grid=(nl, n6 // tn),
        in_specs=[pl.BlockSpec((MOD_ROWS, d), lambda l, n: (0, 0)),
                  pl.BlockSpec((None, d, tn), lambda l, n: (l, 0, n)),
                  pl.BlockSpec((None, 1, tn), lambda l, n: (l, 0, n))],
        out_specs=pl.BlockSpec((None, MOD_ROWS, tn), lambda l, n: (l, 0, n)),
        out_shape=jax.ShapeDtypeStruct((nl, MOD_ROWS, n6), F32),
        compiler_params=_params(("parallel", "parallel"), 4 * d * tn * 4 + 2**22),
        name="ada_mod",
    )(cv, ada_w, ada_b.reshape(nl, 1, n6))


def _modcast_body(x_ref, sc_ref, sh_ref, o_ref):
    o_ref[...] = (x_ref[...] * (1.0 + sc_ref[...]) + sh_ref[...]).astype(BF16)


def _modcast_call(x, mod, g):
    m, d = x.shape
    tm = _tile(math.gcd(g.mp, g.ts), 512)
    row = lambda i: _mod_row(i, tm, g.mp, g.ts)
    return pl.pallas_call(
        _modcast_body,
        grid=(m // tm,),
        in_specs=[pl.BlockSpec((tm, d), lambda i: (i, 0)), _mod_spec(1, d, row), _mod_spec(0, d, row)],
        out_specs=pl.BlockSpec((tm, d), lambda i: (i, 0)),
        out_shape=jax.ShapeDtypeStruct((m, d), BF16),
        compiler_params=_params(("parallel",), 2 * tm * d * 6 + 2**22),
        name="modulate_in",
    )(x, mod, mod)


def _cast_weight_once(w_ref, wb_ref, first):
    @pl.when(first)
    def _():
        wb_ref[...] = w_ref[...].astype(BF16)


def _mla_down_body(h_ref, w_ref, qn_ref, kvn_ref, cos_ref, sin_ref, cq_ref, ckv_ref, kpe_ref, wb_ref,
                   *, rq, rkv):
    _cast_weight_once(w_ref, wb_ref, pl.program_id(0) == 0)
    a = _dot(h_ref[...], wb_ref[...])
    cq_ref[...] = (_rms(a[:, :rq]) * qn_ref[...]).astype(BF16)
    ckv_ref[...] = _rms(a[:, rq:rq + rkv]) * kvn_ref[...]
    kpe_ref[...] = _rope(a[:, rq + rkv:], cos_ref[...], sin_ref[...])


def _mla_down_call(h, w_cat, q_norm, kv_norm, cos_t, sin_t, g):
    m, d = h.shape
    rq, rkv = q_norm.shape[0], kv_norm.shape[0]
    n = w_cat.shape[1]
    tm = _tile(math.gcd(g.mp, g.ts), 512)
    const = lambda i: (0, 0)
    rows = lambda i: (i, 0)
    return pl.pallas_call(
        functools.partial(_mla_down_body, rq=rq, rkv=rkv),
        grid=(m // tm,),
        in_specs=[pl.BlockSpec((tm, d), rows),
                  pl.BlockSpec((d, n), const, pipeline_mode=pl.Buffered(1)),
                  pl.BlockSpec((1, rq), const), pl.BlockSpec((1, rkv), const),
                  pl.BlockSpec((tm, LANE), rows), pl.BlockSpec((tm, LANE), rows)],
        out_specs=[pl.BlockSpec((tm, rq), rows), pl.BlockSpec((tm, rkv), rows), pl.BlockSpec((tm, LANE), rows)],
        out_shape=[jax.ShapeDtypeStruct((m, rq), BF16), jax.ShapeDtypeStruct((m, rkv), F32),
                   jax.ShapeDtypeStruct((m, LANE), F32)],
        scratch_shapes=[pltpu.VMEM((d, n), BF16)],
        compiler_params=_params(("arbitrary",), d * n * 6 + 2 * tm * (d * 2 + n * 8) + 2**23),
        name="mla_down",
    )(h, w_cat, _row_vec(q_norm), _row_vec(kv_norm), cos_t, sin_t)


def _mla_q_body(cq_ref, w_ref, cos_ref, sin_ref, q_ref, wb_ref, *, heads, scale):
    _cast_weight_once(w_ref, wb_ref, pl.program_id(0) == 0)
    a = _dot(cq_ref[...], wb_ref[...])
    hn = heads * LANE
    pe = _rope(a[:, hn:], _lane_tile(cos_ref[...], heads), _lane_tile(sin_ref[...], heads))
    q_ref[:, :hn] = (a[:, :hn] * scale).astype(BF16)
    q_ref[:, hn:] = (pe * scale).astype(BF16)


def _mla_q_call(cq, w_uq_pad, cos_t, sin_t, heads, g):
    m, rq = cq.shape
    n = w_uq_pad.shape[1]
    tm = _tile(math.gcd(g.mp, g.ts), 512)
    const = lambda i: (0, 0)
    rows = lambda i: (i, 0)
    scale = (MLA_NOPE + MLA_ROPE) ** -0.5
    return pl.pallas_call(
        functools.partial(_mla_q_body, heads=heads, scale=scale),
        grid=(m // tm,),
        in_specs=[pl.BlockSpec((tm, rq), rows),
                  pl.BlockSpec((rq, n), const, pipeline_mode=pl.Buffered(1)),
                  pl.BlockSpec((tm, LANE), rows), pl.BlockSpec((tm, LANE), rows)],
        out_specs=pl.BlockSpec((tm, n), rows),
        out_shape=jax.ShapeDtypeStruct((m, n), BF16),
        scratch_shapes=[pltpu.VMEM((rq, n), BF16)],
        compiler_params=_params(("arbitrary",), rq * n * 6 + 2 * tm * (rq * 2 + n * 2) + 4 * tm * n * 4 + 2**23),
        name="mla_q_up",
    )(cq, w_uq_pad, cos_t, sin_t)


def _mm_body(x_ref, w_ref, o_ref, wb_ref):
    _cast_weight_once(w_ref, wb_ref, pl.program_id(0) == 0)
    o_ref[...] = _dot(x_ref[...].astype(BF16), wb_ref[...]).astype(o_ref.dtype)


def _mm_call(x, w, m, out_dtype, name):
    k = x.shape[1]
    n = w.shape[1]
    tm = _tile(m, 512)
    return pl.pallas_call(
        _mm_body,
        grid=(m // tm,),
        in_specs=[pl.BlockSpec((tm, k), lambda i: (i, 0)),
                  pl.BlockSpec((k, n), lambda i: (0, 0), pipeline_mode=pl.Buffered(1))],
        out_specs=pl.BlockSpec((tm, n), lambda i: (i, 0)),
        out_shape=jax.ShapeDtypeStruct((m, n), out_dtype),
        scratch_shapes=[pltpu.VMEM((k, n), BF16)],
        compiler_params=_params(("arbitrary",), k * n * 6 + 2 * tm * (k * 4 + n * 2) + tm * n * 4 + 2**23),
        name=name,
    )(x, w)


def _softmax_parts(s):
    m = jnp.max(s, axis=-1, keepdims=True)
    e = jnp.exp(s - m)
    return e, jnp.sum(e, axis=-1, keepdims=True)


def _mla_head(qn, qp, k, v, kpe):
    s = _dot_nt(qn, k) + _dot_nt(qp, kpe)
    e, l = _softmax_parts(s)
    return _dot(e.astype(BF16), v) * (1.0 / l)


def _mla_attn_ctx_body(qn_ref, qp_ref, kv_ref, kpe_ref, o_ref, *, heads):
    kpe = kpe_ref[...].astype(BF16)
    for h in range(heads):
        lo = h * LANE
        o = _mla_head(qn_ref[:, lo:lo + LANE], qp_ref[:, lo:lo + LANE],
                      kv_ref[:, 2 * lo:2 * lo + LANE], kv_ref[:, 2 * lo + LANE:2 * lo + 2 * LANE], kpe)
        o_ref[:, lo:lo + LANE] = o.astype(BF16)


def _mla_attn_lat_body(qn_ref, qp_ref, k_ref, v_ref, kpe_ref, o_alias, o_ref):
    del o_alias
    o = _mla_head(qn_ref[...], qp_ref[...], k_ref[...], v_ref[...], kpe_ref[...].astype(BF16))
    o_ref[...] = o.astype(BF16)


def _mla_attn(q, kv_p, kpe_p, kv_s, kpe_s, heads, g):
    m = q.shape[0]
    hn = heads * LANE
    tk = kv_s.shape[0] // g.bs
    o = pl.pallas_call(
        functools.partial(_mla_attn_ctx_body, heads=heads),
        grid=(g.b,),
        in_specs=[pl.BlockSpec((g.s, hn), lambda b: (b, 0)), pl.BlockSpec((g.s, hn), lambda b: (b, 1)),
                  pl.BlockSpec((g.s, 2 * hn), lambda b: (b, 0)), pl.BlockSpec((g.s, LANE), lambda b: (b, 0))],
        out_specs=pl.BlockSpec((g.s, hn), lambda b: (b, 0)),
        out_shape=jax.ShapeDtypeStruct((m, hn), BF16),
        compiler_params=_params(("parallel",), 2 * g.s * hn * 12 + 2**24),
        name="mla_attn_ctx",
    )(q, q, kv_p, kpe_p)
    tq = _tile(g.ts, 512)
    qb0 = g.mp // tq
    nq = g.ts // tq
    return pl.pallas_call(
        _mla_attn_lat_body,
        grid=(g.bs, heads, nq),
        in_specs=[pl.BlockSpec((tq, LANE), lambda b, h, i: (qb0 + b * nq + i, h)),
                  pl.BlockSpec((tq, LANE), lambda b, h, i: (qb0 + b * nq + i, heads + h)),
                  pl.BlockSpec((tk, LANE), lambda b, h, i: (b, 2 * h)),
                  pl.BlockSpec((tk, LANE), lambda b, h, i: (b, 2 * h + 1)),
                  pl.BlockSpec((tk, LANE), lambda b, h, i: (b, 0)),
                  pl.BlockSpec(memory_space=pl.ANY)],
        out_specs=pl.BlockSpec((tq, LANE), lambda b, h, i: (qb0 + b * nq + i, h)),
        out_shape=jax.ShapeDtypeStruct((m, hn), BF16),
        input_output_aliases={5: 0},
        compiler_params=_params(("parallel", "parallel", "parallel"), 4 * tq * tk * 4 + 2**24),
        name="mla_attn_lat",
    )(q, q, kv_s, kv_s, kpe_s, o)


def _diff_lambda(lam_ref, lam_init):
    lv = lam_ref[...]
    s1 = jnp.sum(lv[0:1] * lv[1:2], axis=-1, keepdims=True)
    s2 = jnp.sum(lv[2:3] * lv[3:4], axis=-1, keepdims=True)
    return jnp.exp(s1) - jnp.exp(s2) + lam_init


def _diff_head(q, k, v, lam, subln, lam_init):
    lane = lax.broadcasted_iota(jnp.int32, q.shape, 1)
    zero = jnp.zeros_like(q)
    e1, l1 = _softmax_parts(_dot_nt(jnp.where(lane < DIFF_DK, q, zero), k))
    e2, l2 = _softmax_parts(_dot_nt(jnp.where(lane < DIFF_DK, zero, q), k))
    o = _dot(e1.astype(BF16), v) * (1.0 / l1) - _dot(e2.astype(BF16), v) * (lam / l2)
    return _rms(o) * subln * (1.0 - lam_init)


def _diff_attn_ctx_body(q_ref, k_ref, v_ref, lam_ref, sub_ref, o_ref, *, heads, lam_init):
    lam = _diff_lambda(lam_ref, lam_init)
    for h in range(heads):
        lo = h * LANE
        o = _diff_head(q_ref[:, lo:lo + LANE], k_ref[:, lo:lo + LANE].astype(BF16),
                       v_ref[:, lo:lo + LANE].astype(BF16), lam, sub_ref[...], lam_init)
        o_ref[:, lo:lo + LANE] = o.astype(BF16)


def _diff_attn_lat_body(q_ref, k_ref, v_ref, lam_ref, sub_ref, o_alias, o_ref, *, lam_init):
    del o_alias
    lam = _diff_lambda(lam_ref, lam_init)
    o = _diff_head(q_ref[...], k_ref[...].astype(BF16), v_ref[...].astype(BF16), lam, sub_ref[...], lam_init)
    o_ref[...] = o.astype(BF16)


def _diff_attn(q, k_p, v_p, k_s, v_s, lam_vecs, subln, lam_init, heads, g):
    m = q.shape[0]
    hn = heads * LANE
    tk = k_s.shape[0] // g.bs
    sub = _row_vec(subln)
    o = pl.pallas_call(
        functools.partial(_diff_attn_ctx_body, heads=heads, lam_init=lam_init),
        grid=(g.b,),
        in_specs=[pl.BlockSpec((g.s, hn), lambda b: (b, 0)), pl.BlockSpec((g.s, hn), lambda b: (b, 0)),
                  pl.BlockSpec((g.s, hn), lambda b: (b, 0)),
                  pl.BlockSpec((4, DIFF_DK), lambda b: (0, 0)), pl.BlockSpec((1, DIFF_DV), lambda b: (0, 0))],
        out_specs=pl.BlockSpec((g.s, hn), lambda b: (b, 0)),
        out_shape=jax.ShapeDtypeStruct((m, hn), BF16),
        compiler_params=_params(("parallel",), 2 * g.s * hn * 12 + 2**24),
        name="diff_attn_ctx",
    )(q, k_p, v_p, lam_vecs, sub)
    tq = _tile(g.ts, 512)
    qb0 = g.mp // tq
    nq = g.ts // tq
    return pl.pallas_call(
        functools.partial(_diff_attn_lat_body, lam_init=lam_init),
        grid=(g.bs, heads, nq),
        in_specs=[pl.BlockSpec((tq, LANE), lambda b, h, i: (qb0 + b * nq + i, h)),
                  pl.BlockSpec((tk, LANE), lambda b, h, i: (b, h)),
                  pl.BlockSpec((tk, LANE), lambda b, h, i: (b, h)),
                  pl.BlockSpec((4, DIFF_DK), lambda b, h, i: (0, 0)),
                  pl.BlockSpec((1, DIFF_DV), lambda b, h, i: (0, 0)),
                  pl.BlockSpec(memory_space=pl.ANY)],
        out_specs=pl.BlockSpec((tq, LANE), lambda b, h, i: (qb0 + b * nq + i, h)),
        out_shape=jax.ShapeDtypeStruct((m, hn), BF16),
        input_output_aliases={5: 0},
        compiler_params=_params(("parallel", "parallel", "parallel"), 6 * tq * tk * 4 + 2**24),
        name="diff_attn_lat",
    )(q, k_s, v_s, lam_vecs, sub, o)


def _proj_body(h_ref, w_ref, cos_ref, sin_ref, o_ref, wb_ref, *, rope, scale):
    _cast_weight_once(w_ref, wb_ref, pl.program_id(1) == 0)
    a = _dot(h_ref[...], wb_ref[...])
    if rope:
        reps = a.shape[1] // LANE
        a = _rope(a, _lane_tile(cos_ref[...], reps), _lane_tile(sin_ref[...], reps))
    o_ref[...] = (a * scale).astype(o_ref.dtype)


def _proj_call(h, w, col0, ncols, cos_t, sin_t, rope, scale, out_dtype, g, name):
    m, d = h.shape
    tn = _tile(ncols, 1024, LANE)
    tm = _tile(math.gcd(g.mp, g.ts), 512)
    nb0 = col0 // tn
    return pl.pallas_call(
        functools.partial(_proj_body, rope=rope, scale=scale),
        grid=(ncols // tn, m // tm),
        in_specs=[pl.BlockSpec((tm, d), lambda j, i: (i, 0)),
                  pl.BlockSpec((d, tn), lambda j, i: (0, nb0 + j)),
                  pl.BlockSpec((tm, LANE), lambda j, i: (i, 0)), pl.BlockSpec((tm, LANE), lambda j, i: (i, 0))],
        out_specs=pl.BlockSpec((tm, tn), lambda j, i: (i, j)),
        out_shape=jax.ShapeDtypeStruct((m, ncols), out_dtype),
        scratch_shapes=[pltpu.VMEM((d, tn), BF16)],
        compiler_params=_params(("arbitrary", "arbitrary"), d * tn * 10 + 2 * tm * (d * 2 + tn * 4) + 4 * tm * tn * 4 + 2**23),
        name=name,
    )(h, w, cos_t, sin_t)


def _out_ln_body(o_ref, w_ref, x_ref, gate_ref, g_ref, b_ref, sc_ref, sh_ref, xo_ref, ho_ref, wb_ref, *, alpha):
    _cast_weight_once(w_ref, wb_ref, pl.program_id(0) == 0)
    y = _dot(o_ref[...], wb_ref[...])
    xn = _layer_norm(alpha * x_ref[...] + gate_ref[...] * y, g_ref[...], b_ref[...])
    xo_ref[...] = xn
    ho_ref[...] = (xn * (1.0 + sc_ref[...]) + sh_ref[...]).astype(BF16)


def _out_ln_call(o, w_o, x, mod, ln_g, ln_b, alpha, g):
    m, k = o.shape
    d = w_o.shape[1]
    tm = _tile(math.gcd(g.mp, g.ts), 256)
    row = lambda i: _mod_row(i, tm, g.mp, g.ts)
    const = lambda i: (0, 0)
    rows = lambda i: (i, 0)
    return pl.pallas_call(
        functools.partial(_out_ln_body, alpha=alpha),
        grid=(m // tm,),
        in_specs=[pl.BlockSpec((tm, k), rows),
                  pl.BlockSpec((k, d), const, pipeline_mode=pl.Buffered(1)),
                  pl.BlockSpec((tm, d), rows), _mod_spec(2, d, row),
                  pl.BlockSpec((1, d), const), pl.BlockSpec((1, d), const),
                  _mod_spec(4, d, row), _mod_spec(3, d, row)],
        out_specs=[pl.BlockSpec((tm, d), rows), pl.BlockSpec((tm, d), rows)],
        out_shape=[jax.ShapeDtypeStruct((m, d), F32), jax.ShapeDtypeStruct((m, d), BF16)],
        scratch_shapes=[pltpu.VMEM((k, d), BF16)],
        compiler_params=_params(("arbitrary",), k * d * 6 + 2 * tm * (k * 2 + d * 10) + 4 * tm * d * 4 + 2**23),
        name="out_proj_ln",
    )(o, w_o, x, mod, _row_vec(ln_g), _row_vec(ln_b), mod, mod)


def _lane_cumsum(x, tri):
    c = tri.shape[0]
    carry = jnp.zeros((x.shape[0], 1), F32)
    parts = []
    for j in range(x.shape[1] // c):
        xc = x[:, j * c:(j + 1) * c]
        parts.append(_dot(xc.astype(BF16), tri) + carry)
        carry = carry + jnp.sum(xc, axis=1, keepdims=True)
    return parts[0] if len(parts) == 1 else jnp.concatenate(parts, axis=1)


def _route_body(h_ref, wr_ref, slot_ref, gate_ref, *, groups, t, cap):
    logits = _dot_nt(wr_ref[...].astype(BF16), h_ref[...])
    ex = jnp.exp(logits - jnp.max(logits, axis=0, keepdims=True))
    aff = ex / jnp.sum(ex, axis=0, keepdims=True)
    keys = lax.bitcast_convert_type(aff, jnp.int32)
    c = min(t, 2 * LANE)
    tri = jnp.where(lax.broadcasted_iota(jnp.int32, (c, c), 0) <= lax.broadcasted_iota(jnp.int32, (c, c), 1),
                    1.0, 0.0).astype(BF16)
    capf = float(cap)
    for gi in range(groups):
        k = keys[:, gi * t:(gi + 1) * t]
        thr = jnp.zeros((k.shape[0], 1), jnp.int32)
        for bit in range(30, -1, -1):
            cand = thr | (1 << bit)
            cnt = jnp.sum(jnp.where(k >= cand, 1.0, 0.0), axis=1, keepdims=True)
            thr = jnp.where(cnt >= capf, cand, thr)
        gt = jnp.where(k > thr, 1.0, 0.0)
        eq = jnp.where(k == thr, 1.0, 0.0)
        need = capf - jnp.sum(gt, axis=1, keepdims=True)
        eq_before = _lane_cumsum(eq, tri) - eq
        sel = gt + eq * jnp.where(eq_before < need, 1.0, 0.0)
        pos = _lane_cumsum(sel, tri) - 1.0
        slot_ref[gi] = jnp.where(sel > 0.5, pos, -1.0).astype(jnp.int32)
        gate_ref[gi] = aff[:, gi * t:(gi + 1) * t]


def _route_call(h2, w_router_t, row0, nreq, t, cap, groups, name):
    e, d = w_router_t.shape
    rb0 = row0 // (groups * t)
    return pl.pallas_call(
        functools.partial(_route_body, groups=groups, t=t, cap=cap),
        grid=(nreq // groups,),
        in_specs=[pl.BlockSpec((groups * t, d), lambda i: (rb0 + i, 0)), pl.BlockSpec((e, d), lambda i: (0, 0))],
        out_specs=[pl.BlockSpec((groups, e, t), lambda i: (i, 0, 0)), pl.BlockSpec((groups, e, t), lambda i: (i, 0, 0))],
        out_shape=[jax.ShapeDtypeStruct((nreq, e, t), jnp.int32), jax.ShapeDtypeStruct((nreq, e, t), F32)],
        compiler_params=_params(("parallel",), 2 * groups * t * d * 2 + 2**24),
        name=name,
    )(h2, w_router_t)


def _one_hot_rows(slot_row, cap):
    r = lax.broadcasted_iota(jnp.int32, (cap, slot_row.shape[1]), 0)
    return slot_row == r


def _gather_ctx_body(h_ref, slot_ref, gate_ref, xs_ref, gs_ref, *, cap):
    ne = slot_ref.shape[0]
    masks = [_one_hot_rows(slot_ref[e:e + 1, :], cap) for e in range(ne)]
    p = jnp.concatenate([jnp.where(mk, 1.0, 0.0) for mk in masks], axis=0).astype(BF16)
    xs = _dot(p, h_ref[...]).astype(BF16)
    for e in range(ne):
        xs_ref[e] = xs[e * cap:(e + 1) * cap]
        gs_ref[e] = jnp.sum(jnp.where(masks[e], gate_ref[e:e + 1, :], 0.0), axis=1, keepdims=True)


def _gather_lat_body(h_ref, slot_ref, gate_ref, xs_alias, gs_alias, xs_ref, gs_ref, *, cap):
    del xs_alias, gs_alias
    e = pl.program_id(1)
    mk = _one_hot_rows(slot_ref[pl.ds(e, 1), :], cap)
    xs_ref[...] = _dot(jnp.where(mk, 1.0, 0.0).astype(BF16), h_ref[...]).astype(BF16)
    gs_ref[...] = jnp.sum(jnp.where(mk, gate_ref[pl.ds(e, 1), :], 0.0), axis=1, keepdims=True)


def _gather(h2, slot_p, gate_p, slot_s, gate_s, g):
    d = h2.shape[1]
    ne = slot_p.shape[1]
    rows_e = g.b * g.cap_p + g.bs * g.cap_s
    shapes = [jax.ShapeDtypeStruct((ne, rows_e, d), BF16), jax.ShapeDtypeStruct((ne, rows_e, 1), F32)]
    xs, gs = pl.pallas_call(
        functools.partial(_gather_ctx_body, cap=g.cap_p),
        grid=(g.b,),
        in_specs=[pl.BlockSpec((g.s, d), lambda b: (b, 0)),
                  pl.BlockSpec((None, ne, g.s), lambda b: (b, 0, 0)), pl.BlockSpec((None, ne, g.s), lambda b: (b, 0, 0))],
        out_specs=[pl.BlockSpec((ne, g.cap_p, d), lambda b: (0, b, 0)), pl.BlockSpec((ne, g.cap_p, 1), lambda b: (0, b, 0))],
        out_shape=shapes,
        compiler_params=_params(("parallel",), 2 * ne * g.cap_p * (d * 8 + g.s * 8) + 2**24),
        name="moe_gather_ctx",
    )(h2, slot_p, gate_p)
    blk0 = (g.b * g.cap_p) // g.cap_s
    hb0 = g.mp // g.ts
    return pl.pallas_call(
        functools.partial(_gather_lat_body, cap=g.cap_s),
        grid=(g.bs, ne),
        in_specs=[pl.BlockSpec((g.ts, d), lambda b, e: (hb0 + b, 0)),
                  pl.BlockSpec((None, ne, g.ts), lambda b, e: (b, 0, 0)), pl.BlockSpec((None, ne, g.ts), lambda b, e: (b, 0, 0)),
                  pl.BlockSpec(memory_space=pl.ANY), pl.BlockSpec(memory_space=pl.ANY)],
        out_specs=[pl.BlockSpec((None, g.cap_s, d), lambda b, e: (e, blk0 + b, 0)),
                   pl.BlockSpec((None, g.cap_s, 1), lambda b, e: (e, blk0 + b, 0))],
        out_shape=shapes,
        input_output_aliases={3: 0, 4: 1},
        compiler_params=_params(("parallel", "parallel"), 2 * g.ts * d * 2 + g.cap_s * (g.ts * 12 + d * 12) + 2**24),
        name="moe_gather_lat",
    )(h2, slot_s, gate_s, xs, gs)


def _ffn1_body(x_ref, wg_ref, wu_ref, o_ref):
    x = x_ref[...]
    a = _dot(x, wg_ref[...].astype(BF16))
    u = _dot(x, wu_ref[...].astype(BF16))
    o_ref[...] = (a * _sigmoid(a) * u).astype(BF16)


def _ffn1_call(xs, w_gate, w_up):
    ne, rows, d = xs.shape
    f = w_gate.shape[2]
    tn = _tile(f, 256, LANE)
    return pl.pallas_call(
        _ffn1_body,
        grid=(ne, f // tn),
        in_specs=[pl.BlockSpec((None, rows, d), lambda e, n: (e, 0, 0)),
                  pl.BlockSpec((None, d, tn), lambda e, n: (e, 0, n)), pl.BlockSpec((None, d, tn), lambda e, n: (e, 0, n))],
        out_specs=pl.BlockSpec((None, rows, tn), lambda e, n: (e, 0, n)),
        out_shape=jax.ShapeDtypeStruct((ne, rows, f), BF16),
        compiler_params=_params(("parallel", "parallel"), 2 * rows * d * 2 + 4 * d * tn * 4 + 2 * d * tn * 2 + 5 * rows * tn * 4 + 2**23),
        name="moe_ffn_gate_up",
    )(xs, w_gate, w_up)


def _ffn2_body(h_ref, w_ref, gs_ref, o_ref):
    o_ref[...] = (_dot(h_ref[...], w_ref[...].astype(BF16)) * gs_ref[...]).astype(BF16)


def _ffn2_call(hm, w_down, gs):
    ne, rows, f = hm.shape
    d = w_down.shape[2]
    tn = _tile(d, 256, LANE)
    return pl.pallas_call(
        _ffn2_body,
        grid=(ne, d // tn),
        in_specs=[pl.BlockSpec((None, rows, f), lambda e, n: (e, 0, 0)),
                  pl.BlockSpec((None, f, tn), lambda e, n: (e, 0, n)),
                  pl.BlockSpec((None, rows, 1), lambda e, n: (e, 0, 0))],
        out_specs=pl.BlockSpec((None, rows, tn), lambda e, n: (e, 0, n)),
        out_shape=jax.ShapeDtypeStruct((ne, rows, d), BF16),
        compiler_params=_params(("parallel", "parallel"), 2 * rows * f * 2 + 2 * f * tn * 4 + f * tn * 2 + 3 * rows * tn * 4 + 2 * rows * LANE * 4 + 2**23),
        name="moe_ffn_down",
    )(hm, w_down, gs)


def _combine(y_ref, slot, cap):
    ne = y_ref.shape[0]
    p = jnp.concatenate([jnp.where(_one_hot_rows(slot[e:e + 1, :], cap), 1.0, 0.0) for e in range(ne)], axis=0)
    y = y_ref[...].reshape(ne * cap, y_ref.shape[2])
    return _dot_tn(p.astype(BF16), y)


def _combine_ln_body(*refs, cap, alpha, n_alias, emit_h):
    y_ref, slot_ref, x_ref, gate_ref, g_ref, b_ref = refs[:6]
    rest = refs[6:]
    if emit_h:
        sc_ref, sh_ref = rest[:2]
        rest = rest[2:]
    outs = rest[n_alias:]
    moe = _combine(y_ref, slot_ref[...], cap)
    xn = _layer_norm(alpha * x_ref[...] + gate_ref[...] * moe, g_ref[...], b_ref[...])
    outs[0][...] = xn
    if emit_h:
        outs[1][...] = (xn * (1.0 + sc_ref[...]) + sh_ref[...]).astype(BF16)


def _combine_ln(y, slot_p, slot_s, x, mod, mod_next, ln_g, ln_b, alpha, g):
    ne, rows_e, d = y.shape
    m = x.shape[0]
    emit_h = mod_next is not None
    lg, lb = _row_vec(ln_g), _row_vec(ln_b)
    shapes = [jax.ShapeDtypeStruct((m, d), F32)] + ([jax.ShapeDtypeStruct((m, d), BF16)] if emit_h else [])
    n_out = len(shapes)

    def specs(row):
        sp = [pl.BlockSpec((1, d), lambda *a: (0, 0)), pl.BlockSpec((1, d), lambda *a: (0, 0))]
        md = [_mod_spec(5, d, row)]
        nx = [_mod_spec(1, d, row), _mod_spec(0, d, row)] if emit_h else []
        return md, sp, nx

    md, sp, nx = specs(lambda b: 0)
    outs = pl.pallas_call(
        functools.partial(_combine_ln_body, cap=g.cap_p, alpha=alpha, n_alias=0, emit_h=emit_h),
        grid=(g.b,),
        in_specs=[pl.BlockSpec((ne, g.cap_p, d), lambda b: (0, b, 0)),
                  pl.BlockSpec((None, ne, g.s), lambda b: (b, 0, 0)),
                  pl.BlockSpec((g.s, d), lambda b: (b, 0))] + md + sp + nx,
        out_specs=[pl.BlockSpec((g.s, d), lambda b: (b, 0))] * n_out,
        out_shape=shapes,
        compiler_params=_params(("parallel",), 2 * ne * g.cap_p * (d * 2 + g.s * 8) + g.s * d * 40 + 2**24),
        name="moe_combine_ln_ctx",
    )(y, slot_p, x, mod, lg, lb, *([mod_next, mod_next] if emit_h else []))
    outs = list(outs) if isinstance(outs, (list, tuple)) else [outs]
    tt = _tile(g.ts, 256, LANE)
    nt = g.ts // tt
    blk0 = (g.b * g.cap_p) // g.cap_s
    xb0 = g.mp // tt
    md, sp, nx = specs(lambda b, i: 1 + b)
    n_in = 6 + len(nx)
    outs = pl.pallas_call(
        functools.partial(_combine_ln_body, cap=g.cap_s, alpha=alpha, n_alias=n_out, emit_h=emit_h),
        grid=(g.bs, nt),
        in_specs=[pl.BlockSpec((ne, g.cap_s, d), lambda b, i: (0, blk0 + b, 0), pipeline_mode=pl.Buffered(1)),
                  pl.BlockSpec((None, ne, tt), lambda b, i: (b, 0, i)),
                  pl.BlockSpec((tt, d), lambda b, i: (xb0 + b * nt + i, 0))] + md + sp + nx
                 + [pl.BlockSpec(memory_space=pl.ANY)] * n_out,
        out_specs=[pl.BlockSpec((tt, d), lambda b, i: (xb0 + b * nt + i, 0))] * n_out,
        out_shape=shapes,
        input_output_aliases={n_in + j: j for j in range(n_out)},
        compiler_params=_params(("parallel", "parallel"), ne * g.cap_s * (d * 2 + tt * 10) + tt * d * 48 + 2**24),
        name="moe_combine_ln_lat",
    )(y, slot_s, x, mod, lg, lb, *([mod_next, mod_next] if emit_h else []), *outs)
    outs = list(outs) if isinstance(outs, (list, tuple)) else [outs]
    return outs[0], (outs[1] if emit_h else None)


class _Geom:
    def __init__(self, b, s, bs, ts, ne):
        self.b, self.s, self.bs, self.ts = b, s, bs, ts
        self.mp, self.ms = b * s, bs * ts
        self.cap_p = CAP_FACTOR * s // ne
        self.cap_s = CAP_FACTOR * ts // ne
        assert ts % s == 0 and self.mp % ts == 0 and bs + 1 <= MOD_ROWS
        assert (b * self.cap_p) % self.cap_s == 0 and self.cap_p % 16 == 0


def _rope_tables(g):
    rows = g.ts // GRID_W
    row = jnp.repeat(jnp.arange(rows, dtype=F32), GRID_W)
    col = jnp.tile(jnp.arange(GRID_W, dtype=F32), rows)
    half = ROT_DIM // 2
    inv_freq = ROPE_BASE ** (-jnp.arange(0, half, 2, dtype=F32) / half)
    ang_r = row[:, None] * inv_freq
    ang_c = col[:, None] * inv_freq
    ang = jnp.concatenate([ang_r, ang_r, ang_c, ang_c] * (LANE // ROT_DIM), axis=-1)
    cos = jnp.concatenate([jnp.ones((g.mp, LANE), F32), jnp.tile(jnp.cos(ang), (g.bs, 1))], axis=0)
    sin = jnp.concatenate([jnp.zeros((g.mp, LANE), F32), jnp.tile(jnp.sin(ang), (g.bs, 1))], axis=0)
    return cos, sin


def _pad_heads(w, heads, lo, width):
    k = w.shape[0]
    per = w.shape[1] // heads
    part = w.reshape(k, heads, per)[:, :, lo:lo + width]
    return jnp.pad(part, ((0, 0), (0, 0), (0, LANE - width))).reshape(k, heads * LANE)


def kernel(x_prompt, x_sample, cache_mla_ckv, cache_mla_kpe, cache_diff_k, cache_diff_v, c, c_ctx, ada_w, ada_b, ln1_g, ln1_b, ln2_g, ln2_b, mla_w_dq, mla_q_norm, mla_w_uq, mla_w_dkv, mla_kv_norm, mla_w_ukv, mla_w_o, diff_w_qkv, diff_lambda_q1, diff_lambda_k1, diff_lambda_q2, diff_lambda_k2, diff_subln, diff_w_o, moe_w_router, moe_w_gate, moe_w_up, moe_w_down):
    b, s, d = x_prompt.shape
    bs, ts, _ = x_sample.shape
    depth = ada_w.shape[0]
    ne = moe_w_router.shape[-1]
    g = _Geom(b, s, bs, ts, ne)
    mla_heads = mla_w_o.shape[1] // MLA_V
    diff_heads = diff_w_o.shape[1] // DIFF_DV
    alpha = (2.0 * depth) ** 0.25
    past = cache_mla_ckv.shape[2]

    x = jnp.concatenate([x_prompt.reshape(g.mp, d), x_sample.reshape(g.ms, d)], axis=0)
    cv = jnp.concatenate([c_ctx[None, :], c, jnp.zeros((MOD_ROWS - 1 - bs, d), F32)], axis=0)
    mods = _mod_call(cv, ada_w, ada_b)
    mod_of = lambda l: mods[l].reshape(MOD_ROWS, 1, 6 * d)
    cos_t, sin_t = _rope_tables(g)
    h = _modcast_call(x, mod_of(0), g)

    st_ckv, st_kpe, st_dk, st_dv = [], [], [], []
    for l in range(depth):
        mod = mod_of(l)
        j = l // 2
        if l % 2 == 0:
            w_down = jnp.pad(jnp.concatenate([mla_w_dq[j], mla_w_dkv[j]], axis=1), ((0, 0), (0, LANE - MLA_ROPE)))
            cq, ckv, kpe = _mla_down_call(h, w_down, mla_q_norm[j], mla_kv_norm[j], cos_t, sin_t, g)
            per_q = MLA_NOPE + MLA_ROPE
            w_uq = jnp.concatenate([_pad_heads(mla_w_uq[j], mla_heads, 0, MLA_NOPE),
                                    _pad_heads(mla_w_uq[j], mla_heads, MLA_NOPE, per_q - MLA_NOPE)], axis=1)
            q = _mla_q_call(cq, w_uq, cos_t, sin_t, mla_heads, g)
            st_ckv.append(ckv[:g.mp].reshape(b, s, -1))
            st_kpe.append(kpe[:g.mp, :MLA_ROPE].reshape(b, s, MLA_ROPE))
            ckv_s = jnp.concatenate([cache_mla_ckv[:, j], ckv[g.mp:].reshape(bs, ts, -1)], axis=1)
            kpe_c = jnp.pad(cache_mla_kpe[:, j], ((0, 0), (0, 0), (0, LANE - MLA_ROPE)))
            kpe_s = jnp.concatenate([kpe_c, kpe[g.mp:].reshape(bs, ts, LANE)], axis=1).reshape(bs * (past + ts), LANE)
            kv_p = _mm_call(ckv, mla_w_ukv[j], g.mp, BF16, "mla_kv_up_ctx")
            kv_s = _mm_call(ckv_s.reshape(bs * (past + ts), -1), mla_w_ukv[j], bs * (past + ts), BF16, "mla_kv_up_lat")
            o = _mla_attn(q, kv_p, kpe, kv_s, kpe_s, mla_heads, g)
            w_o = mla_w_o[j]
        else:
            lam_init = 0.8 - 0.6 * math.exp(-0.3 * l)
            nq = diff_heads * 2 * DIFF_DK
            w = diff_w_qkv[j]
            q = _proj_call(h, w, 0, nq, cos_t, sin_t, True, DIFF_DK ** -0.5, BF16, g, "diff_q")
            k = _proj_call(h, w, nq, nq, cos_t, sin_t, True, 1.0, F32, g, "diff_k")
            v = _proj_call(h, w, 2 * nq, diff_heads * DIFF_DV, cos_t, sin_t, False, 1.0, F32, g, "diff_v")
            st_dk.append(k[:g.mp].reshape(b, s, diff_heads, 2, DIFF_DK))
            st_dv.append(v[:g.mp].reshape(b, s, diff_heads, DIFF_DV))
            k_s = jnp.concatenate([cache_diff_k[:, j].reshape(bs, past, nq), k[g.mp:].reshape(bs, ts, nq)], axis=1)
            v_s = jnp.concatenate([cache_diff_v[:, j].reshape(bs, past, -1), v[g.mp:].reshape(bs, ts, -1)], axis=1)
            lam_vecs = jnp.stack([diff_lambda_q1[j], diff_lambda_k1[j], diff_lambda_q2[j], diff_lambda_k2[j]])
            o = _diff_attn(q, k, v, k_s.reshape(bs * (past + ts), nq), v_s.reshape(bs * (past + ts), -1),
                           lam_vecs, diff_subln[j], lam_init, diff_heads, g)
            w_o = diff_w_o[j]
        x, h2 = _out_ln_call(o, w_o, x, mod, ln1_g[l], ln1_b[l], alpha, g)

        w_rt = moe_w_router[l].T
        groups = max(1, min(g.b, g.ts // g.s))
        slot_p, gate_p = _route_call(h2, w_rt, 0, g.b, g.s, g.cap_p, groups, "moe_route_ctx")
        slot_s, gate_s = _route_call(h2, w_rt, g.mp, g.bs, g.ts, g.cap_s, 1, "moe_route_lat")
        xs, gs = _gather(h2, slot_p, gate_p, slot_s, gate_s, g)
        hm = _ffn1_call(xs, moe_w_gate[l], moe_w_up[l])
        y = _ffn2_call(hm, moe_w_down[l], gs)
        mod_next = mod_of(l + 1) if l + 1 < depth else None
        x, h = _combine_ln(y, slot_p, slot_s, x, mod, mod_next, ln2_g[l], ln2_b[l], alpha, g)

    y_prompt = x[:g.mp].reshape(b, s, d)
    y_sample = x[g.mp:].reshape(bs, ts, d)
    return (y_prompt, y_sample, jnp.stack(st_ckv, axis=1), jnp.stack(st_kpe, axis=1),
            jnp.stack(st_dk, axis=1), jnp.stack(st_dv, axis=1))
```

```python
import functools
import math

import jax
import jax.numpy as jnp
from jax import lax
from jax.experimental import pallas as pl
from jax.experimental.pallas import tpu as pltpu

F32 = jnp.float32
BF16 = jnp.bfloat16

NORM_EPS = 1e-6
ROPE_BASE = 10000.0
GRID_W = 64
CAP_FACTOR = 2
MLA_NOPE, MLA_ROPE, MLA_V = 128, 64, 128
DIFF_DK, DIFF_DV = 64, 128
ROT_DIM = 64
LANE = 128
MOD_ROWS = 8
LOG2E = math.log2(math.e)
SUBLANE = 8
Q_TILE = 256
KV_CHUNK = 256
V7X_VMEM_BYTES = 64 * 2**20
VMEM_HEADROOM_BYTES = 6 * 2**20


def _params(semantics, vmem_bytes):
    limit = min(int(vmem_bytes), V7X_VMEM_BYTES - VMEM_HEADROOM_BYTES)
    return pltpu.CompilerParams(dimension_semantics=semantics, vmem_limit_bytes=limit)


def _tile(n, target, mult=16):
    t = min(n, target)
    while t > mult and (n % t or t % mult):
        t -= mult
    assert n % t == 0 and t % mult == 0, (n, target, mult)
    return t


def _dot(a, b):
    return jnp.dot(a, b, preferred_element_type=F32)


def _dot_nt(a, b):
    return lax.dot_general(a, b, (((1,), (1,)), ((), ())), preferred_element_type=F32)


def _dot_tn(a, b):
    return lax.dot_general(a, b, (((0,), (0,)), ((), ())), preferred_element_type=F32)


def _sigmoid(x):
    return 1.0 / (1.0 + jnp.exp(-x))


def _rms(x):
    return x * lax.rsqrt(jnp.mean(x * x, axis=-1, keepdims=True) + NORM_EPS)


def _layer_norm(z, g, b):
    mu = jnp.mean(z, axis=-1, keepdims=True)
    zc = z - mu
    var = jnp.mean(zc * zc, axis=-1, keepdims=True)
    return zc * lax.rsqrt(var + NORM_EPS) * g + b


def _rope(x, cos, sin):
    n = x.shape[-1]
    nxt = pltpu.roll(x, n - 16, 1)
    prv = pltpu.roll(x, 16, 1)
    lane = lax.broadcasted_iota(jnp.int32, x.shape, 1)
    rot = jnp.where((lane % 32) < 16, -nxt, prv)
    return x * cos + rot * sin


def _lane_tile(t, reps):
    return t if reps == 1 else jnp.concatenate([t] * reps, axis=-1)


def _mod_row(i, tm, mp, ts):
    r0 = i * tm
    return jnp.where(r0 < mp, 0, 1 + (r0 - mp) // ts)


def _mod_spec(which, d, row_fn):
    return pl.BlockSpec((None, 1, d), lambda *ids: (row_fn(*ids), 0, which))


def _row_vec(v):
    return v.reshape(1, -1)


def _mod_body(c_ref, w_ref, b_ref, o_ref):
    c = c_ref[...]
    s = (c * _sigmoid(c)).astype(BF16)
    o_ref[...] = _dot(s, w_ref[...].astype(BF16)) + b_ref[...]


def _mod_call(cv, ada_w, ada_b):
    nl, d, n6 = ada_w.shape
    tn = _tile(n6, 1024, LANE)
    return pl.pallas_call(
        _mod_body,
        grid=(nl, n6 // tn),
        in_specs=[pl.BlockSpec((MOD_ROWS, d), lambda l, n: (0, 0)),
                  pl.BlockSpec((None, d, tn), lambda l, n: (l, 0, n)),
                  pl.BlockSpec((None, 1, tn), lambda l, n: (l, 0, n))],
        out_specs=pl.BlockSpec((None, MOD_ROWS, tn), lambda l, n: (l, 0, n)),
        out_shape=jax.ShapeDtypeStruct((nl, MOD_ROWS, n6), F32),
        compiler_params=_params(("parallel", "parallel"), 4 * d * tn * 4 + 2**22),
        name="ada_mod",
    )(cv, ada_w, ada_b.reshape(nl, 1, n6))


def _modcast_body(x_ref, sc_ref, sh_ref, o_ref):
    o_ref[...] = (x_ref[...] * (1.0 + sc_ref[...]) + sh_ref[...]).astype(BF16)


def _modcast_call(x, mod, g):
    m, d = x.shape
    tm = _tile(math.gcd(g.mp, g.ts), 512)
    row = lambda i: _mod_row(i, tm, g.mp, g.ts)
    return pl.pallas_call(
        _modcast_body,
        grid=(m // tm,),
        in_specs=[pl.BlockSpec((tm, d), lambda i: (i, 0)), _mod_spec(1, d, row), _mod_spec(0, d, row)],
        out_specs=pl.BlockSpec((tm, d), lambda i: (i, 0)),
        out_shape=jax.ShapeDtypeStruct((m, d), BF16),
        compiler_params=_params(("parallel",), 2 * tm * d * 6 + 2**22),
        name="modulate_in",
    )(x, mod, mod)


def _cast_weight_once(w_ref, wb_ref, first):
    @pl.when(first)
    def _():
        wb_ref[...] = w_ref[...].astype(BF16)


def _mla_down_body(h_ref, w_ref, qn_ref, kvn_ref, cos_ref, sin_ref, cq_ref, ckv_ref, kpe_ref, kpe16_ref, wb_ref,
                   *, rq, rkv):
    _cast_weight_once(w_ref, wb_ref, pl.program_id(0) == 0)
    a = _dot(h_ref[...], wb_ref[...])
    cq_ref[...] = (_rms(a[:, :rq]) * qn_ref[...]).astype(BF16)
    ckv_ref[...] = _rms(a[:, rq:rq + rkv]) * kvn_ref[...]
    kpe = _rope(a[:, rq + rkv:], cos_ref[...], sin_ref[...])
    kpe_ref[...] = kpe
    kpe16_ref[...] = kpe.astype(BF16)


def _mla_down_call(h, w_cat, q_norm, kv_norm, cos_t, sin_t, g):
    m, d = h.shape
    rq, rkv = q_norm.shape[0], kv_norm.shape[0]
    n = w_cat.shape[1]
    tm = _tile(math.gcd(g.mp, g.ts), 512)
    const = lambda i: (0, 0)
    rows = lambda i: (i, 0)
    return pl.pallas_call(
        functools.partial(_mla_down_body, rq=rq, rkv=rkv),
        grid=(m // tm,),
        in_specs=[pl.BlockSpec((tm, d), rows),
                  pl.BlockSpec((d, n), const, pipeline_mode=pl.Buffered(1)),
                  pl.BlockSpec((1, rq), const), pl.BlockSpec((1, rkv), const),
                  pl.BlockSpec((tm, LANE), rows), pl.BlockSpec((tm, LANE), rows)],
        out_specs=[pl.BlockSpec((tm, rq), rows), pl.BlockSpec((tm, rkv), rows), pl.BlockSpec((tm, LANE), rows),
                   pl.BlockSpec((tm, LANE), rows)],
        out_shape=[jax.ShapeDtypeStruct((m, rq), BF16), jax.ShapeDtypeStruct((m, rkv), F32),
                   jax.ShapeDtypeStruct((m, LANE), F32), jax.ShapeDtypeStruct((m, LANE), BF16)],
        scratch_shapes=[pltpu.VMEM((d, n), BF16)],
        compiler_params=_params(("arbitrary",), d * n * 6 + 2 * tm * (d * 2 + n * 8) + 2**23),
        name="mla_down",
    )(h, w_cat, _row_vec(q_norm), _row_vec(kv_norm), cos_t, sin_t)


def _mla_q_body(cq_ref, w_ref, cos_ref, sin_ref, q_ref, wb_ref, *, heads, scale):
    _cast_weight_once(w_ref, wb_ref, pl.program_id(0) == 0)
    a = _dot(cq_ref[...], wb_ref[...])
    cos, sin = cos_ref[...], sin_ref[...]
    for h in range(heads):
        lo = 2 * LANE * h
        q_ref[:, lo:lo + LANE] = (a[:, lo:lo + LANE] * scale).astype(BF16)
        q_ref[:, lo + LANE:lo + 2 * LANE] = (_rope(a[:, lo + LANE:lo + 2 * LANE], cos, sin) * scale).astype(BF16)


def _mla_q_call(cq, w_uq_pad, cos_t, sin_t, heads, g):
    m, rq = cq.shape
    n = w_uq_pad.shape[1]
    tm = _tile(math.gcd(g.mp, g.ts), 512)
    const = lambda i: (0, 0)
    rows = lambda i: (i, 0)
    scale = (MLA_NOPE + MLA_ROPE) ** -0.5 * LOG2E
    return pl.pallas_call(
        functools.partial(_mla_q_body, heads=heads, scale=scale),
        grid=(m // tm,),
        in_specs=[pl.BlockSpec((tm, rq), rows),
                  pl.BlockSpec((rq, n), const, pipeline_mode=pl.Buffered(1)),
                  pl.BlockSpec((tm, LANE), rows), pl.BlockSpec((tm, LANE), rows)],
        out_specs=pl.BlockSpec((tm, n), rows),
        out_shape=jax.ShapeDtypeStruct((m, n), BF16),
        scratch_shapes=[pltpu.VMEM((rq, n), BF16)],
        compiler_params=_params(("arbitrary",), rq * n * 6 + 2 * tm * (rq * 2 + n * 2) + 4 * tm * n * 4 + 2**23),
        name="mla_q_up",
    )(cq, w_uq_pad, cos_t, sin_t)


def _kv_up_body(x_ref, kpe_ref, w_ref, kcat_ref, vt_ref, wb_ref, *, heads):
    _cast_weight_once(w_ref, wb_ref, pl.program_id(0) == 0)
    a = _dot(x_ref[...].astype(BF16), wb_ref[...])
    kpe = kpe_ref[...]
    for h in range(heads):
        lo = 2 * LANE * h
        kcat_ref[:, lo:lo + LANE] = a[:, lo:lo + LANE].astype(BF16)
        kcat_ref[:, lo + LANE:lo + 2 * LANE] = kpe
        vt_ref[h * LANE:(h + 1) * LANE, :] = a[:, lo + LANE:lo + 2 * LANE].T.astype(BF16)


def _kv_up_call(x, kpe16, w_all, j, heads, name):
    m, k = x.shape
    n = w_all.shape[2]
    tm = _tile(m, 512)
    rows = lambda i: (i, 0)
    return pl.pallas_call(
        functools.partial(_kv_up_body, heads=heads),
        grid=(m // tm,),
        in_specs=[pl.BlockSpec((tm, k), rows), pl.BlockSpec((tm, LANE), rows),
                  pl.BlockSpec((None, k, n), lambda i: (j, 0, 0), pipeline_mode=pl.Buffered(1))],
        out_specs=[pl.BlockSpec((tm, 2 * heads * LANE), rows), pl.BlockSpec((heads * LANE, tm), lambda i: (0, i))],
        out_shape=[jax.ShapeDtypeStruct((m, 2 * heads * LANE), BF16), jax.ShapeDtypeStruct((heads * LANE, m), BF16)],
        scratch_shapes=[pltpu.VMEM((k, n), BF16)],
        compiler_params=_params(("arbitrary",), k * n * 6 + 2 * tm * (k * 4 + n * 3) + tm * n * 4 + 2**23),
        name=name,
    )(x, kpe16, w_all)


def _row_fold(x, op):
    out = x[:SUBLANE]
    for r in range(1, x.shape[0] // SUBLANE):
        out = op(out, x[r * SUBLANE:(r + 1) * SUBLANE])
    return out


def _chunks(segments):
    out = []
    for k_ref, vt_ref in segments:
        n = k_ref.shape[0]
        assert n % LANE == 0
        out += [(k_ref, vt_ref, r0, min(KV_CHUNK, n - r0)) for r0 in range(0, n, KV_CHUNK)]
    return out


def _scores_pass(qt, segments, s_ref):
    m = None
    off = 0
    for k_ref, _, r0, n in _chunks(segments):
        s = _dot(k_ref[r0:r0 + n, :], qt)
        s_ref[off:off + n, :] = s
        cm = _row_fold(s, jnp.maximum)
        m = cm if m is None else jnp.maximum(m, cm)
        off += n
    return jnp.max(m, axis=0, keepdims=True)


def _values_pass(segments, s_ref, m):
    acc = l = None
    off = 0
    for _, vt_ref, r0, n in _chunks(segments):
        e = jnp.exp2(s_ref[off:off + n, :] - m)
        cl = _row_fold(e, jnp.add)
        pv = _dot(vt_ref[:, r0:r0 + n], e.astype(BF16))
        l = cl if l is None else l + cl
        acc = pv if acc is None else acc + pv
        off += n
    return acc * (1.0 / jnp.sum(l, axis=0, keepdims=True))


def _pipelined_tiles(n_tiles, scores_of, values_of, s_a, s_b):
    if n_tiles == 1:
        values_of(0, s_a, scores_of(0, s_a))
        return
    assert n_tiles % 2 == 0

    def body(j, m_a):
        t = 2 * j
        m_b = scores_of(t + 1, s_b)
        values_of(t, s_a, m_a)
        m_next = scores_of(jnp.minimum(t + 2, n_tiles - 1), s_a)
        values_of(t + 1, s_b, m_b)
        return m_next

    lax.fori_loop(0, n_tiles // 2, body, scores_of(0, s_a))


def _tile_rows(t, n):
    return pl.ds(t * n, n) if isinstance(t, int) else pl.ds(pl.multiple_of(t * n, n), n)


def _transpose_bf16(x):
    return x.astype(F32).T.astype(BF16)


def _pipelined_static(tiles, scores_of, values_of, s_a, s_b):
    m_prev = None
    for idx, tile in enumerate(tiles):
        m_cur = scores_of(tile, (s_a, s_b)[idx % 2])
        if idx:
            values_of(tiles[idx - 1], (s_a, s_b)[(idx - 1) % 2], m_prev)
        m_prev = m_cur
    values_of(tiles[-1], (s_a, s_b)[(len(tiles) - 1) % 2], m_prev)


def _mla_attn_ctx_body(q_ref, k_ref, vt_ref, o_ref, s_a, s_b, *, heads):
    nq = s_a.shape[1]
    seg = lambda h: [(k_ref.at[:, 2 * LANE * h:2 * LANE * (h + 1)], vt_ref.at[LANE * h:LANE * (h + 1), :])]

    def scores_of(tile, s_ref):
        h, r0 = tile
        return _scores_pass(_transpose_bf16(q_ref[r0:r0 + nq, 2 * LANE * h:2 * LANE * (h + 1)]), seg(h), s_ref)

    def values_of(tile, s_ref, m):
        h, r0 = tile
        o_ref[r0:r0 + nq, LANE * h:LANE * (h + 1)] = _values_pass(seg(h), s_ref, m).T.astype(BF16)

    tiles = [(h, r0) for h in range(heads) for r0 in range(0, q_ref.shape[0], nq)]
    _pipelined_static(tiles, scores_of, values_of, s_a, s_b)


def _mla_attn_lat_body(q_ref, kc_ref, vc_ref, kn_ref, vn_ref, o_alias, o_ref, s_a, s_b):
    del o_alias
    segs = [(kc_ref, vc_ref), (kn_ref, vn_ref)]

    def scores_of(t, s_ref):
        return _scores_pass(_transpose_bf16(q_ref[_tile_rows(t, Q_TILE), :]), segs, s_ref)

    def values_of(t, s_ref, m):
        o_ref[_tile_rows(t, Q_TILE), :] = _values_pass(segs, s_ref, m).T.astype(BF16)

    _pipelined_tiles(q_ref.shape[0] // Q_TILE, scores_of, values_of, s_a, s_b)


def _mla_attn(q, kcat, vt, kcat_c, vt_c, heads, g):
    m = q.shape[0]
    hn = heads * LANE
    past = kcat_c.shape[0] // g.bs
    o = pl.pallas_call(
        functools.partial(_mla_attn_ctx_body, heads=heads),
        grid=(g.b,),
        in_specs=[pl.BlockSpec((g.s, 2 * hn), lambda b: (b, 0)), pl.BlockSpec((g.s, 2 * hn), lambda b: (b, 0)),
                  pl.BlockSpec((hn, g.s), lambda b: (0, b))],
        out_specs=pl.BlockSpec((g.s, hn), lambda b: (b, 0)),
        out_shape=jax.ShapeDtypeStruct((m, hn), BF16),
        scratch_shapes=[pltpu.VMEM((g.s, min(Q_TILE, g.s)), F32)] * 2,
        compiler_params=_params(("arbitrary",), 2 * g.s * hn * 12 + 2**24),
        name="mla_attn_ctx",
    )(q, kcat, vt)
    kb0 = g.mp // g.ts
    return pl.pallas_call(
        _mla_attn_lat_body,
        grid=(g.bs, heads),
        in_specs=[pl.BlockSpec((g.ts, 2 * LANE), lambda b, h: (kb0 + b, h)),
                  pl.BlockSpec((past, 2 * LANE), lambda b, h: (b, h)),
                  pl.BlockSpec((LANE, past), lambda b, h: (h, b)),
                  pl.BlockSpec((g.ts, 2 * LANE), lambda b, h: (kb0 + b, h)),
                  pl.BlockSpec((LANE, g.ts), lambda b, h: (h, kb0 + b)),
                  pl.BlockSpec(memory_space=pl.ANY)],
        out_specs=pl.BlockSpec((g.ts, LANE), lambda b, h: (kb0 + b, h)),
        out_shape=jax.ShapeDtypeStruct((m, hn), BF16),
        input_output_aliases={5: 0},
        scratch_shapes=[pltpu.VMEM((past + g.ts, Q_TILE), F32)] * 2,
        compiler_params=_params(("arbitrary", "arbitrary"), 2**25),
        name="mla_attn_lat",
    )(q, kcat_c, vt_c, kcat, vt, o)


def _diff_lambda(lam_ref, lam_init):
    lv = lam_ref[...]
    s1 = jnp.sum(lv[0:1] * lv[1:2], axis=-1, keepdims=True)
    s2 = jnp.sum(lv[2:3] * lv[3:4], axis=-1, keepdims=True)
    return jnp.exp(s1) - jnp.exp(s2) + lam_init


def _diff_queries_t(q):
    qt = q.astype(F32).T
    row = lax.broadcasted_iota(jnp.int32, qt.shape, 0)
    return jnp.concatenate([jnp.where(row < DIFF_DK, qt, 0.0), jnp.where(row < DIFF_DK, 0.0, qt)],
                           axis=1).astype(BF16)


def _diff_combine(ot, lam, subln_col, lam_init):
    n = ot.shape[1] // 2
    dt = ot[:, :n] - lam * ot[:, n:]
    dt = dt * lax.rsqrt(jnp.mean(dt * dt, axis=0, keepdims=True) + NORM_EPS) * subln_col * (1.0 - lam_init)
    return dt.T.astype(BF16)


def _diff_attn_ctx_body(q_ref, k_ref, vt_ref, lam_ref, sub_ref, o_ref, s_a, s_b, *, heads, lam_init):
    lam = _diff_lambda(lam_ref, lam_init)
    nq = s_a.shape[1] // 2
    cols = lambda h: slice(LANE * h, LANE * (h + 1))
    seg = lambda h: [(k_ref.at[:, cols(h)], vt_ref.at[cols(h), :])]

    def scores_of(tile, s_ref):
        h, r0 = tile
        return _scores_pass(_diff_queries_t(q_ref[r0:r0 + nq, cols(h)]), seg(h), s_ref)

    def values_of(tile, s_ref, m):
        h, r0 = tile
        o_ref[r0:r0 + nq, cols(h)] = _diff_combine(_values_pass(seg(h), s_ref, m), lam, sub_ref[...], lam_init)

    tiles = [(h, r0) for h in range(heads) for r0 in range(0, q_ref.shape[0], nq)]
    _pipelined_static(tiles, scores_of, values_of, s_a, s_b)


def _diff_attn_lat_body(q_ref, kc_ref, vc_ref, kn_ref, vn_ref, lam_ref, sub_ref, o_alias, o_ref, s_a, s_b, *,
                        lam_init):
    del o_alias
    lam = _diff_lambda(lam_ref, lam_init)
    segs = [(kc_ref, vc_ref), (kn_ref, vn_ref)]
    nq = Q_TILE // 2

    def scores_of(t, s_ref):
        return _scores_pass(_diff_queries_t(q_ref[_tile_rows(t, nq), :]), segs, s_ref)

    def values_of(t, s_ref, m):
        o_ref[_tile_rows(t, nq), :] = _diff_combine(_values_pass(segs, s_ref, m), lam, sub_ref[...], lam_init)

    _pipelined_tiles(q_ref.shape[0] // nq, scores_of, values_of, s_a, s_b)


def _diff_attn(q, k, vt, k_c, vt_c, lam_vecs, subln, lam_init, heads, g):
    m = q.shape[0]
    hn = heads * LANE
    past = k_c.shape[0] // g.bs
    sub = subln.reshape(-1, 1)
    blk = lambda b: (b, 0)
    o = pl.pallas_call(
        functools.partial(_diff_attn_ctx_body, heads=heads, lam_init=lam_init),
        grid=(g.b,),
        in_specs=[pl.BlockSpec((g.s, hn), blk), pl.BlockSpec((g.s, hn), blk), pl.BlockSpec((hn, g.s), lambda b: (0, b)),
                  pl.BlockSpec((4, DIFF_DK), lambda b: (0, 0)), pl.BlockSpec((DIFF_DV, 1), lambda b: (0, 0))],
        out_specs=pl.BlockSpec((g.s, hn), blk),
        out_shape=jax.ShapeDtypeStruct((m, hn), BF16),
        scratch_shapes=[pltpu.VMEM((g.s, min(Q_TILE, 2 * g.s)), F32)] * 2,
        compiler_params=_params(("arbitrary",), 2 * g.s * hn * 12 + 2**24),
        name="diff_attn_ctx",
    )(q, k, vt, lam_vecs, sub)
    kb0 = g.mp // g.ts
    return pl.pallas_call(
        functools.partial(_diff_attn_lat_body, lam_init=lam_init),
        grid=(g.bs, heads),
        in_specs=[pl.BlockSpec((g.ts, LANE), lambda b, h: (kb0 + b, h)),
                  pl.BlockSpec((past, LANE), lambda b, h: (b, h)),
                  pl.BlockSpec((LANE, past), lambda b, h: (h, b)),
                  pl.BlockSpec((g.ts, LANE), lambda b, h: (kb0 + b, h)),
                  pl.BlockSpec((LANE, g.ts), lambda b, h: (h, kb0 + b)),
                  pl.BlockSpec((4, DIFF_DK), lambda b, h: (0, 0)),
                  pl.BlockSpec((DIFF_DV, 1), lambda b, h: (0, 0)),
                  pl.BlockSpec(memory_space=pl.ANY)],
        out_specs=pl.BlockSpec((g.ts, LANE), lambda b, h: (kb0 + b, h)),
        out_shape=jax.ShapeDtypeStruct((m, hn), BF16),
        input_output_aliases={7: 0},
        scratch_shapes=[pltpu.VMEM((past + g.ts, Q_TILE), F32)] * 2,
        compiler_params=_params(("arbitrary", "arbitrary"), 2**25),
        name="diff_attn_lat",
    )(q, k_c, vt_c, k, vt, lam_vecs, sub, o)


def _proj_body(h_ref, w_ref, cos_ref, sin_ref, *refs, rope, scale, transposed):
    out_refs, wb_ref = refs[:-1], refs[-1]
    _cast_weight_once(w_ref, wb_ref, pl.program_id(1) == 0)
    a = _dot(h_ref[...], wb_ref[...])
    if rope:
        reps = a.shape[1] // LANE
        a = _rope(a, _lane_tile(cos_ref[...], reps), _lane_tile(sin_ref[...], reps))
    if scale != 1.0:
        a = a * scale
    for o_ref, tr in zip(out_refs, transposed):
        o_ref[...] = (a.T if tr else a).astype(o_ref.dtype)


def _proj_call(h, w_all, layer, col0, ncols, cos_t, sin_t, rope, scale, outs, g, name):
    m, d = h.shape
    tn = _tile(ncols, 1024, LANE)
    tm = _tile(math.gcd(g.mp, g.ts), 512, LANE)
    nb0 = col0 // tn
    return pl.pallas_call(
        functools.partial(_proj_body, rope=rope, scale=scale, transposed=tuple(tr for _, tr in outs)),
        grid=(ncols // tn, m // tm),
        in_specs=[pl.BlockSpec((tm, d), lambda j, i: (i, 0)),
                  pl.BlockSpec((None, d, tn), lambda j, i: (layer, 0, nb0 + j)),
                  pl.BlockSpec((tm, LANE), lambda j, i: (i, 0)), pl.BlockSpec((tm, LANE), lambda j, i: (i, 0))],
        out_specs=[pl.BlockSpec((tn, tm), lambda j, i: (j, i)) if tr else pl.BlockSpec((tm, tn), lambda j, i: (i, j))
                   for _, tr in outs],
        out_shape=[jax.ShapeDtypeStruct((ncols, m) if tr else (m, ncols), dt) for dt, tr in outs],
        scratch_shapes=[pltpu.VMEM((d, tn), BF16)],
        compiler_params=_params(("arbitrary", "arbitrary"),
                                d * tn * 10 + 2 * tm * (d * 2 + tn * 6) + 4 * tm * tn * 4 + 2**23),
        name=name,
    )(h, w_all, cos_t, sin_t)


def _out_ln_body(o_ref, w_ref, x_ref, gate_ref, g_ref, b_ref, sc_ref, sh_ref, xo_ref, ho_ref, wb_ref, *, alpha):
    _cast_weight_once(w_ref, wb_ref, pl.program_id(0) == 0)
    y = _dot(o_ref[...], wb_ref[...])
    xn = _layer_norm(alpha * x_ref[...] + gate_ref[...] * y, g_ref[...], b_ref[...])
    xo_ref[...] = xn
    ho_ref[...] = (xn * (1.0 + sc_ref[...]) + sh_ref[...]).astype(BF16)


def _out_ln_call(o, w_o_all, layer, x, mod, ln_g, ln_b, alpha, g):
    m, k = o.shape
    d = w_o_all.shape[2]
    tm = _tile(math.gcd(g.mp, g.ts), 256)
    row = lambda i: _mod_row(i, tm, g.mp, g.ts)
    const = lambda i: (0, 0)
    rows = lambda i: (i, 0)
    return pl.pallas_call(
        functools.partial(_out_ln_body, alpha=alpha),
        grid=(m // tm,),
        in_specs=[pl.BlockSpec((tm, k), rows),
                  pl.BlockSpec((None, k, d), lambda i: (layer, 0, 0), pipeline_mode=pl.Buffered(1)),
                  pl.BlockSpec((tm, d), rows), _mod_spec(2, d, row),
                  pl.BlockSpec((1, d), const), pl.BlockSpec((1, d), const),
                  _mod_spec(4, d, row), _mod_spec(3, d, row)],
        out_specs=[pl.BlockSpec((tm, d), rows), pl.BlockSpec((tm, d), rows)],
        out_shape=[jax.ShapeDtypeStruct((m, d), F32), jax.ShapeDtypeStruct((m, d), BF16)],
        scratch_shapes=[pltpu.VMEM((k, d), BF16)],
        compiler_params=_params(("arbitrary",), k * d * 6 + 2 * tm * (k * 2 + d * 10) + 4 * tm * d * 4 + 2**23),
        name="out_proj_ln",
    )(o, w_o_all, x, mod, _row_vec(ln_g), _row_vec(ln_b), mod, mod)


def _lane_cumsum(x, tri):
    c = tri.shape[0]
    carry = jnp.zeros((x.shape[0], 1), F32)
    parts = []
    for j in range(x.shape[1] // c):
        xc = x[:, j * c:(j + 1) * c]
        parts.append(_dot(xc.astype(BF16), tri) + carry)
        carry = carry + jnp.sum(xc, axis=1, keepdims=True)
    return parts[0] if len(parts) == 1 else jnp.concatenate(parts, axis=1)


def _route_body(h_ref, wr_ref, slot_ref, gate_ref, *, groups, t, cap):
    logits = _dot_nt(wr_ref[...].astype(BF16), h_ref[...])
    ex = jnp.exp(logits - jnp.max(logits, axis=0, keepdims=True))
    aff = ex / jnp.sum(ex, axis=0, keepdims=True)
    nbits = jnp.finfo(aff.dtype).bits
    keys = lax.bitcast_convert_type(aff, jnp.dtype(f"int{nbits}"))
    c = min(t, 2 * LANE)
    tri = jnp.where(lax.broadcasted_iota(jnp.int32, (c, c), 0) <= lax.broadcasted_iota(jnp.int32, (c, c), 1),
                    1.0, 0.0).astype(BF16)
    capf = float(cap)
    for gi in range(groups):
        k = keys[:, gi * t:(gi + 1) * t]
        thr = jnp.zeros((k.shape[0], 1), keys.dtype)
        for bit in range(nbits - 2, -1, -1):
            cand = thr | (1 << bit)
            cnt = jnp.sum(jnp.where(k >= cand, 1.0, 0.0), axis=1, keepdims=True)
            thr = jnp.where(cnt >= capf, cand, thr)
        gt = jnp.where(k > thr, 1.0, 0.0)
        eq = jnp.where(k == thr, 1.0, 0.0)
        need = capf - jnp.sum(gt, axis=1, keepdims=True)
        eq_before = _lane_cumsum(eq, tri) - eq
        sel = gt + eq * jnp.where(eq_before < need, 1.0, 0.0)
        pos = _lane_cumsum(sel, tri) - 1.0
        slot_ref[gi] = jnp.where(sel > 0.5, pos, -1.0).astype(jnp.int32)
        gate_ref[gi] = aff[:, gi * t:(gi + 1) * t]


def _route_call(h2, w_router_t, row0, nreq, t, cap, groups, name):
    e, d = w_router_t.shape
    rb0 = row0 // (groups * t)
    return pl.pallas_call(
        functools.partial(_route_body, groups=groups, t=t, cap=cap),
        grid=(nreq // groups,),
        in_specs=[pl.BlockSpec((groups * t, d), lambda i: (rb0 + i, 0)), pl.BlockSpec((e, d), lambda i: (0, 0))],
        out_specs=[pl.BlockSpec((groups, e, t), lambda i: (i, 0, 0)), pl.BlockSpec((groups, e, t), lambda i: (i, 0, 0))],
        out_shape=[jax.ShapeDtypeStruct((nreq, e, t), jnp.int32), jax.ShapeDtypeStruct((nreq, e, t), F32)],
        compiler_params=_params(("parallel",), 2 * groups * t * d * 2 + 2**24),
        name=name,
    )(h2, w_router_t)


def _one_hot_rows(slot_row, cap):
    r = lax.broadcasted_iota(jnp.int32, (cap, slot_row.shape[1]), 0)
    return slot_row == r


def _gather_ctx_body(h_ref, slot_ref, gate_ref, xs_ref, gs_ref, *, cap):
    ne = slot_ref.shape[0]
    masks = [_one_hot_rows(slot_ref[e:e + 1, :], cap) for e in range(ne)]
    p = jnp.concatenate([jnp.where(mk, 1.0, 0.0) for mk in masks], axis=0).astype(BF16)
    xs = _dot(p, h_ref[...]).astype(BF16)
    for e in range(ne):
        xs_ref[e] = xs[e * cap:(e + 1) * cap]
        gs_ref[e] = jnp.sum(jnp.where(masks[e], gate_ref[e:e + 1, :], 0.0), axis=1, keepdims=True)


def _gather_lat_body(h_ref, slot_ref, gate_ref, xs_alias, gs_alias, xs_ref, gs_ref, *, cap):
    del xs_alias, gs_alias
    e = pl.program_id(1)
    mk = _one_hot_rows(slot_ref[pl.ds(e, 1), :], cap)
    xs_ref[...] = _dot(jnp.where(mk, 1.0, 0.0).astype(BF16), h_ref[...]).astype(BF16)
    gs_ref[...] = jnp.sum(jnp.where(mk, gate_ref[pl.ds(e, 1), :], 0.0), axis=1, keepdims=True)


def _gather(h2, slot_p, gate_p, slot_s, gate_s, g):
    d = h2.shape[1]
    ne = slot_p.shape[1]
    rows_e = g.b * g.cap_p + g.bs * g.cap_s
    shapes = [jax.ShapeDtypeStruct((ne, rows_e, d), BF16), jax.ShapeDtypeStruct((ne, rows_e, 1), F32)]
    xs, gs = pl.pallas_call(
        functools.partial(_gather_ctx_body, cap=g.cap_p),
        grid=(g.b,),
        in_specs=[pl.BlockSpec((g.s, d), lambda b: (b, 0)),
                  pl.BlockSpec((None, ne, g.s), lambda b: (b, 0, 0)), pl.BlockSpec((None, ne, g.s), lambda b: (b, 0, 0))],
        out_specs=[pl.BlockSpec((ne, g.cap_p, d), lambda b: (0, b, 0)), pl.BlockSpec((ne, g.cap_p, 1), lambda b: (0, b, 0))],
        out_shape=shapes,
        compiler_params=_params(("parallel",), 2 * ne * g.cap_p * (d * 8 + g.s * 8) + 2**24),
        name="moe_gather_ctx",
    )(h2, slot_p, gate_p)
    blk0 = (g.b * g.cap_p) // g.cap_s
    hb0 = g.mp // g.ts
    return pl.pallas_call(
        functools.partial(_gather_lat_body, cap=g.cap_s),
        grid=(g.bs, ne),
        in_specs=[pl.BlockSpec((g.ts, d), lambda b, e: (hb0 + b, 0)),
                  pl.BlockSpec((None, ne, g.ts), lambda b, e: (b, 0, 0)), pl.BlockSpec((None, ne, g.ts), lambda b, e: (b, 0, 0)),
                  pl.BlockSpec(memory_space=pl.ANY), pl.BlockSpec(memory_space=pl.ANY)],
        out_specs=[pl.BlockSpec((None, g.cap_s, d), lambda b, e: (e, blk0 + b, 0)),
                   pl.BlockSpec((None, g.cap_s, 1), lambda b, e: (e, blk0 + b, 0))],
        out_shape=shapes,
        input_output_aliases={3: 0, 4: 1},
        compiler_params=_params(("parallel", "parallel"), 2 * g.ts * d * 2 + g.cap_s * (g.ts * 12 + d * 12) + 2**24),
        name="moe_gather_lat",
    )(h2, slot_s, gate_s, xs, gs)


def _ffn1_body(x_ref, wg_ref, wu_ref, o_ref):
    x = x_ref[...]
    a = _dot(x, wg_ref[...].astype(BF16))
    u = _dot(x, wu_ref[...].astype(BF16))
    o_ref[...] = (a * _sigmoid(a) * u).astype(BF16)


def _ffn1_call(xs, w_gate, w_up, layer):
    ne, rows, d = xs.shape
    f = w_gate.shape[3]
    tn = _tile(f, 256, LANE)
    wspec = pl.BlockSpec((None, None, d, tn), lambda e, n: (layer, e, 0, n))
    return pl.pallas_call(
        _ffn1_body,
        grid=(ne, f // tn),
        in_specs=[pl.BlockSpec((None, rows, d), lambda e, n: (e, 0, 0)), wspec, wspec],
        out_specs=pl.BlockSpec((None, rows, tn), lambda e, n: (e, 0, n)),
        out_shape=jax.ShapeDtypeStruct((ne, rows, f), BF16),
        compiler_params=_params(("parallel", "parallel"), 2 * rows * d * 2 + 4 * d * tn * 4 + 2 * d * tn * 2 + 5 * rows * tn * 4 + 2**23),
        name="moe_ffn_gate_up",
    )(xs, w_gate, w_up)


def _ffn2_body(h_ref, w_ref, gs_ref, o_ref):
    o_ref[...] = (_dot(h_ref[...], w_ref[...].astype(BF16)) * gs_ref[...]).astype(BF16)


def _ffn2_call(hm, w_down, gs, layer):
    ne, rows, f = hm.shape
    d = w_down.shape[3]
    tn = _tile(d, 256, LANE)
    return pl.pallas_call(
        _ffn2_body,
        grid=(ne, d // tn),
        in_specs=[pl.BlockSpec((None, rows, f), lambda e, n: (e, 0, 0)),
                  pl.BlockSpec((None, None, f, tn), lambda e, n: (layer, e, 0, n)),
                  pl.BlockSpec((None, rows, 1), lambda e, n: (e, 0, 0))],
        out_specs=pl.BlockSpec((None, rows, tn), lambda e, n: (e, 0, n)),
        out_shape=jax.ShapeDtypeStruct((ne, rows, d), BF16),
        compiler_params=_params(("parallel", "parallel"), 2 * rows * f * 2 + 2 * f * tn * 4 + f * tn * 2 + 3 * rows * tn * 4 + 2 * rows * LANE * 4 + 2**23),
        name="moe_ffn_down",
    )(hm, w_down, gs)


def _combine(y_ref, slot, cap):
    ne = y_ref.shape[0]
    p = jnp.concatenate([jnp.where(_one_hot_rows(slot[e:e + 1, :], cap), 1.0, 0.0) for e in range(ne)], axis=0)
    y = y_ref[...].reshape(ne * cap, y_ref.shape[2])
    return _dot_tn(p.astype(BF16), y)


def _combine_ln_body(*refs, cap, alpha, n_alias, emit_h):
    y_ref, slot_ref, x_ref, gate_ref, g_ref, b_ref = refs[:6]
    rest = refs[6:]
    if emit_h:
        sc_ref, sh_ref = rest[:2]
        rest = rest[2:]
    outs = rest[n_alias:]
    moe = _combine(y_ref, slot_ref[...], cap)
    xn = _layer_norm(alpha * x_ref[...] + gate_ref[...] * moe, g_ref[...], b_ref[...])
    outs[0][...] = xn
    if emit_h:
        outs[1][...] = (xn * (1.0 + sc_ref[...]) + sh_ref[...]).astype(BF16)


def _combine_ln(y, slot_p, slot_s, x, mod, mod_next, ln_g, ln_b, alpha, g):
    ne, rows_e, d = y.shape
    m = x.shape[0]
    emit_h = mod_next is not None
    lg, lb = _row_vec(ln_g), _row_vec(ln_b)
    shapes = [jax.ShapeDtypeStruct((m, d), F32), jax.ShapeDtypeStruct((m, d), BF16)] if emit_h else \
        [jax.ShapeDtypeStruct((g.mp, d), F32)]
    n_out = len(shapes)

    def specs(row):
        sp = [pl.BlockSpec((1, d), lambda *a: (0, 0)), pl.BlockSpec((1, d), lambda *a: (0, 0))]
        md = [_mod_spec(5, d, row)]
        nx = [_mod_spec(1, d, row), _mod_spec(0, d, row)] if emit_h else []
        return md, sp, nx

    md, sp, nx = specs(lambda b: 0)
    outs = pl.pallas_call(
        functools.partial(_combine_ln_body, cap=g.cap_p, alpha=alpha, n_alias=0, emit_h=emit_h),
        grid=(g.b,),
        in_specs=[pl.BlockSpec((ne, g.cap_p, d), lambda b: (0, b, 0)),
                  pl.BlockSpec((None, ne, g.s), lambda b: (b, 0, 0)),
                  pl.BlockSpec((g.s, d), lambda b: (b, 0))] + md + sp + nx,
        out_specs=[pl.BlockSpec((g.s, d), lambda b: (b, 0))] * n_out,
        out_shape=shapes,
        compiler_params=_params(("parallel",), 2 * ne * g.cap_p * (d * 2 + g.s * 8) + g.s * d * 40 + 2**24),
        name="moe_combine_ln_ctx",
    )(y, slot_p, x, mod, lg, lb, *([mod_next, mod_next] if emit_h else []))
    outs = list(outs) if isinstance(outs, (list, tuple)) else [outs]
    tt = _tile(g.ts, 256, LANE)
    nt = g.ts // tt
    blk0 = (g.b * g.cap_p) // g.cap_s
    xb0 = g.mp // tt
    md, sp, nx = specs(lambda b, i: 1 + b)
    n_in = 6 + len(nx)
    n_alias = n_out if emit_h else 0
    ob0 = xb0 if emit_h else 0
    outs_s = pl.pallas_call(
        functools.partial(_combine_ln_body, cap=g.cap_s, alpha=alpha, n_alias=n_alias, emit_h=emit_h),
        grid=(g.bs, nt),
        in_specs=[pl.BlockSpec((ne, g.cap_s, d), lambda b, i: (0, blk0 + b, 0), pipeline_mode=pl.Buffered(1)),
                  pl.BlockSpec((None, ne, tt), lambda b, i: (b, 0, i)),
                  pl.BlockSpec((tt, d), lambda b, i: (xb0 + b * nt + i, 0))] + md + sp + nx
                 + [pl.BlockSpec(memory_space=pl.ANY)] * n_alias,
        out_specs=[pl.BlockSpec((tt, d), lambda b, i: (ob0 + b * nt + i, 0))] * n_out,
        out_shape=shapes if emit_h else [jax.ShapeDtypeStruct((g.ms, d), F32)],
        input_output_aliases={n_in + j: j for j in range(n_alias)},
        compiler_params=_params(("parallel", "parallel"), ne * g.cap_s * (d * 2 + tt * 10) + tt * d * 48 + 2**24),
        name="moe_combine_ln_lat",
    )(y, slot_s, x, mod, lg, lb, *([mod_next, mod_next] if emit_h else []), *(outs if emit_h else []))
    outs_s = list(outs_s) if isinstance(outs_s, (list, tuple)) else [outs_s]
    if emit_h:
        return outs_s[0], outs_s[1]
    return outs[0], outs_s[0]


class _Geom:
    def __init__(self, b, s, bs, ts, ne):
        self.b, self.s, self.bs, self.ts = b, s, bs, ts
        self.mp, self.ms = b * s, bs * ts
        self.cap_p = CAP_FACTOR * s // ne
        self.cap_s = CAP_FACTOR * ts // ne
        assert ts % s == 0 and self.mp % ts == 0 and bs + 1 <= MOD_ROWS
        assert (b * self.cap_p) % self.cap_s == 0 and self.cap_p % 16 == 0


def _rope_tables(g):
    rows = g.ts // GRID_W
    row = jnp.repeat(jnp.arange(rows, dtype=F32), GRID_W)
    col = jnp.tile(jnp.arange(GRID_W, dtype=F32), rows)
    half = ROT_DIM // 2
    inv_freq = ROPE_BASE ** (-jnp.arange(0, half, 2, dtype=F32) / half)
    ang_r = row[:, None] * inv_freq
    ang_c = col[:, None] * inv_freq
    ang = jnp.concatenate([ang_r, ang_r, ang_c, ang_c] * (LANE // ROT_DIM), axis=-1)
    cos = jnp.concatenate([jnp.ones((g.mp, LANE), F32), jnp.tile(jnp.cos(ang), (g.bs, 1))], axis=0)
    sin = jnp.concatenate([jnp.zeros((g.mp, LANE), F32), jnp.tile(jnp.sin(ang), (g.bs, 1))], axis=0)
    return cos, sin


def _pad_heads(w, heads, width):
    k = w.shape[0]
    per = w.shape[1] // heads
    return jnp.pad(w.reshape(k, heads, per), ((0, 0), (0, 0), (0, width - per))).reshape(k, heads * width)


def kernel(x_prompt, x_sample, cache_mla_ckv, cache_mla_kpe, cache_diff_k, cache_diff_v, c, c_ctx, ada_w, ada_b, ln1_g, ln1_b, ln2_g, ln2_b, mla_w_dq, mla_q_norm, mla_w_uq, mla_w_dkv, mla_kv_norm, mla_w_ukv, mla_w_o, diff_w_qkv, diff_lambda_q1, diff_lambda_k1, diff_lambda_q2, diff_lambda_k2, diff_subln, diff_w_o, moe_w_router, moe_w_gate, moe_w_up, moe_w_down):
    b, s, d = x_prompt.shape
    bs, ts, _ = x_sample.shape
    depth = ada_w.shape[0]
    ne = moe_w_router.shape[-1]
    g = _Geom(b, s, bs, ts, ne)
    mla_heads = mla_w_o.shape[1] // MLA_V
    diff_heads = diff_w_o.shape[1] // DIFF_DV
    alpha = (2.0 * depth) ** 0.25
    past = cache_mla_ckv.shape[2]

    x = jnp.concatenate([x_prompt.reshape(g.mp, d), x_sample.reshape(g.ms, d)], axis=0)
    cv = jnp.concatenate([c_ctx[None, :], c, jnp.zeros((MOD_ROWS - 1 - bs, d), F32)], axis=0)
    mods = _mod_call(cv, ada_w, ada_b)
    mod_of = lambda l: mods[l].reshape(MOD_ROWS, 1, 6 * d)
    cos_t, sin_t = _rope_tables(g)
    h = _modcast_call(x, mod_of(0), g)

    st_ckv, st_kpe, st_dk, st_dv = [], [], [], []
    for l in range(depth):
        mod = mod_of(l)
        j = l // 2
        if l % 2 == 0:
            w_down = jnp.pad(jnp.concatenate([mla_w_dq[j], mla_w_dkv[j]], axis=1), ((0, 0), (0, LANE - MLA_ROPE)))
            cq, ckv, kpe, kpe16 = _mla_down_call(h, w_down, mla_q_norm[j], mla_kv_norm[j], cos_t, sin_t, g)
            q = _mla_q_call(cq, _pad_heads(mla_w_uq[j], mla_heads, 2 * LANE), cos_t, sin_t, mla_heads, g)
            st_ckv.append(ckv[:g.mp].reshape(b, s, -1))
            st_kpe.append(kpe[:g.mp, :MLA_ROPE].reshape(b, s, MLA_ROPE))
            kpe_c = jnp.pad(cache_mla_kpe[:, j], ((0, 0), (0, 0), (0, LANE - MLA_ROPE))).astype(BF16)
            kcat, vt = _kv_up_call(ckv, kpe16, mla_w_ukv, j, mla_heads, "mla_kv_up")
            kcat_c, vt_c = _kv_up_call(cache_mla_ckv[:, j].reshape(bs * past, -1), kpe_c.reshape(bs * past, LANE),
                                       mla_w_ukv, j, mla_heads, "mla_kv_up_cache")
            o = _mla_attn(q, kcat, vt, kcat_c, vt_c, mla_heads, g)
            w_o = mla_w_o
        else:
            lam_init = 0.8 - 0.6 * math.exp(-0.3 * l)
            nq = diff_heads * 2 * DIFF_DK
            nv = diff_heads * DIFF_DV
            q, = _proj_call(h, diff_w_qkv, j, 0, nq, cos_t, sin_t, True, DIFF_DK ** -0.5 * LOG2E,
                            [(BF16, False)], g, "diff_q")
            k, k16 = _proj_call(h, diff_w_qkv, j, nq, nq, cos_t, sin_t, True, 1.0,
                                [(F32, False), (BF16, False)], g, "diff_k")
            v, vt16 = _proj_call(h, diff_w_qkv, j, 2 * nq, nv, cos_t, sin_t, False, 1.0,
                                 [(F32, False), (BF16, True)], g, "diff_v")
            st_dk.append(k[:g.mp].reshape(b, s, diff_heads, 2, DIFF_DK))
            st_dv.append(v[:g.mp].reshape(b, s, diff_heads, DIFF_DV))
            k_c = cache_diff_k[:, j].reshape(bs * past, nq).astype(BF16)
            vt_c = cache_diff_v[:, j].reshape(bs * past, nv).T.astype(BF16)
            lam_vecs = jnp.stack([diff_lambda_q1[j], diff_lambda_k1[j], diff_lambda_q2[j], diff_lambda_k2[j]])
            o = _diff_attn(q, k16, vt16, k_c, vt_c, lam_vecs, diff_subln[j], lam_init, diff_heads, g)
            w_o = diff_w_o
        x, h2 = _out_ln_call(o, w_o, j, x, mod, ln1_g[l], ln1_b[l], alpha, g)

        w_rt = moe_w_router[l].T
        groups = max(1, min(g.b, g.ts // g.s))
        slot_p, gate_p = _route_call(h2, w_rt, 0, g.b, g.s, g.cap_p, groups, "moe_route_ctx")
        slot_s, gate_s = _route_call(h2, w_rt, g.mp, g.bs, g.ts, g.cap_s, 1, "moe_route_lat")
        xs, gs = _gather(h2, slot_p, gate_p, slot_s, gate_s, g)
        hm = _ffn1_call(xs, moe_w_gate, moe_w_up, l)
        y = _ffn2_call(hm, moe_w_down, gs, l)
        mod_next = mod_of(l + 1) if l + 1 < depth else None
        x, h = _combine_ln(y, slot_p, slot_s, x, mod, mod_next, ln2_g[l], ln2_b[l], alpha, g)

    return (x.reshape(b, s, d), h.reshape(bs, ts, d), jnp.stack(st_ckv, axis=1), jnp.stack(st_kpe, axis=1),
            jnp.stack(st_dk, axis=1), jnp.stack(st_dv, axis=1))
```

```python
import functools
import math

import jax
import jax.numpy as jnp
from jax import lax
from jax.experimental import pallas as pl
from jax.experimental.pallas import tpu as pltpu

F32 = jnp.float32
BF16 = jnp.bfloat16

NORM_EPS = 1e-6
ROPE_BASE = 10000.0
GRID_W = 64
CAP_FACTOR = 2
MLA_NOPE, MLA_ROPE, MLA_V = 128, 64, 128
DIFF_DK, DIFF_DV = 64, 128
ROT_DIM = 64
LANE = 128
MOD_ROWS = 8
LOG2E = math.log2(math.e)
SUBLANE = 8
Q_TILE = 256
KV_CHUNK = 256
ROW_SPLIT = 2
V7X_VMEM_BYTES = 64 * 2**20
VMEM_HEADROOM_BYTES = 6 * 2**20


def _params(semantics, vmem_bytes):
    limit = min(int(vmem_bytes), V7X_VMEM_BYTES - VMEM_HEADROOM_BYTES)
    return pltpu.CompilerParams(dimension_semantics=semantics, vmem_limit_bytes=limit)


def _tile(n, target, mult=16):
    t = min(n, target)
    while t > mult and (n % t or t % mult):
        t -= mult
    assert n % t == 0 and t % mult == 0, (n, target, mult)
    return t


def _dot(a, b):
    return jnp.dot(a, b, preferred_element_type=F32)


def _dot_nt(a, b):
    return lax.dot_general(a, b, (((1,), (1,)), ((), ())), preferred_element_type=F32)


def _dot_tn(a, b):
    return lax.dot_general(a, b, (((0,), (0,)), ((), ())), preferred_element_type=F32)


def _sigmoid(x):
    return 1.0 / (1.0 + jnp.exp(-x))


def _rms(x):
    return x * lax.rsqrt(jnp.mean(x * x, axis=-1, keepdims=True) + NORM_EPS)


def _layer_norm(z, g, b):
    mu = jnp.mean(z, axis=-1, keepdims=True)
    zc = z - mu
    var = jnp.mean(zc * zc, axis=-1, keepdims=True)
    return zc * lax.rsqrt(var + NORM_EPS) * g + b


def _rope(x, cos, sin):
    n = x.shape[-1]
    nxt = pltpu.roll(x, n - 16, 1)
    prv = pltpu.roll(x, 16, 1)
    lane = lax.broadcasted_iota(jnp.int32, x.shape, 1)
    rot = jnp.where((lane % 32) < 16, -nxt, prv)
    return x * cos + rot * sin


def _lane_tile(t, reps):
    return t if reps == 1 else jnp.concatenate([t] * reps, axis=-1)


def _mod_row(i, tm, mp, ts):
    r0 = i * tm
    return jnp.where(r0 < mp, 0, 1 + (r0 - mp) // ts)


def _mod_spec(which, d, row_fn):
    return pl.BlockSpec((None, 1, d), lambda *ids: (row_fn(*ids), 0, which))


def _row_vec(v):
    return v.reshape(1, -1)


def _mod_body(c_ref, w_ref, b_ref, o_ref):
    c = c_ref[...]
    s = (c * _sigmoid(c)).astype(BF16)
    o_ref[...] = _dot(s, w_ref[...].astype(BF16)) + b_ref[...]


def _mod_call(cv, ada_w, ada_b):
    nl, d, n6 = ada_w.shape
    tn = _tile(n6, 1024, LANE)
    return pl.pallas_call(
        _mod_body,
        grid=(nl, n6 // tn),
        in_specs=[pl.BlockSpec((MOD_ROWS, d), lambda l, n: (0, 0)),
                  pl.BlockSpec((None, d, tn), lambda l, n: (l, 0, n)),
                  pl.BlockSpec((None, 1, tn), lambda l, n: (l, 0, n))],
        out_specs=pl.BlockSpec((None, MOD_ROWS, tn), lambda l, n: (l, 0, n)),
        out_shape=jax.ShapeDtypeStruct((nl, MOD_ROWS, n6), F32),
        compiler_params=_params(("parallel", "parallel"), 4 * d * tn * 4 + 2**22),
        name="ada_mod",
    )(cv, ada_w, ada_b.reshape(nl, 1, n6))


def _modcast_body(x_ref, sc_ref, sh_ref, o_ref):
    o_ref[...] = (x_ref[...] * (1.0 + sc_ref[...]) + sh_ref[...]).astype(BF16)


def _modcast_call(x, mod, g):
    m, d = x.shape
    tm = _tile(math.gcd(g.mp, g.ts), 512)
    row = lambda i: _mod_row(i, tm, g.mp, g.ts)
    return pl.pallas_call(
        _modcast_body,
        grid=(m // tm,),
        in_specs=[pl.BlockSpec((tm, d), lambda i: (i, 0)), _mod_spec(1, d, row), _mod_spec(0, d, row)],
        out_specs=pl.BlockSpec((tm, d), lambda i: (i, 0)),
        out_shape=jax.ShapeDtypeStruct((m, d), BF16),
        compiler_params=_params(("parallel",), 2 * tm * d * 6 + 2**22),
        name="modulate_in",
    )(x, mod, mod)


def _row_windows(tm, unit):
    n = ROW_SPLIT if tm % (ROW_SPLIT * unit) == 0 else 1
    return [slice(r, r + tm // n) for r in range(0, tm, tm // n)]


def _store_state(st_ref, win, seq, value, transposed):
    for r in range((win.stop - win.start) // seq):
        tok = slice(r * seq, (r + 1) * seq)
        st_ref[win.start // seq + r] = value[:, tok] if transposed else value[tok]


def _state_spec(tm, seq, n_ctx_tiles, layer, rows, cols, transposed, col_block):
    shape = (tm // seq, None, cols, rows) if transposed else (tm // seq, None, rows, cols)

    def index(*ids):
        i, jn = col_block(*ids)
        tile = jnp.minimum(i, n_ctx_tiles - 1)
        return (tile, layer, jn, 0) if transposed else (tile, layer, 0, jn)

    return pl.BlockSpec(shape, index)


def _mla_down_body(h_ref, w_ref, qn_ref, kvn_ref, cos_ref, sin_ref, *refs, rq, rkv, seq, n_ctx_tiles, n_alias):
    cq_ref, ckv_ref, kpe16_ref, st_ckv_ref, st_kpe_ref = refs[n_alias:]
    i = pl.program_id(0)
    wins = _row_windows(h_ref.shape[0], seq)
    accs = [_dot(h_ref[w, :], w_ref[...]) for w in wins]
    for w, a in zip(wins, accs):
        cq_ref[w, :] = (_rms(a[:, :rq]) * qn_ref[...]).astype(BF16)
        ckv = _rms(a[:, rq:rq + rkv]) * kvn_ref[...]
        ckv_ref[w, :] = ckv
        kpe = _rope(a[:, rq + rkv:], cos_ref[w, :], sin_ref[w, :])
        kpe16_ref[w, :] = kpe.astype(BF16)

        @pl.when(i < n_ctx_tiles)
        def _():
            _store_state(st_ckv_ref, w, seq, ckv, False)
            _store_state(st_kpe_ref, w, seq, kpe.T[:MLA_ROPE], True)


def _mla_down_call(h, w_cat, q_norm, kv_norm, cos_t, sin_t, layer, n_layers, states, g):
    m, d = h.shape
    rq, rkv = q_norm.shape[0], kv_norm.shape[0]
    n = w_cat.shape[1]
    tm = _tile(math.gcd(g.mp, g.ts), 512)
    assert tm % g.s == 0
    nct = g.mp // tm
    const = lambda i: (0, 0)
    rows = lambda i: (i, 0)
    n_alias = 0 if states is None else 2
    colb = lambda i: (i, 0)
    return pl.pallas_call(
        functools.partial(_mla_down_body, rq=rq, rkv=rkv, seq=g.s, n_ctx_tiles=nct, n_alias=n_alias),
        grid=(m // tm,),
        in_specs=[pl.BlockSpec((tm, d), rows),
                  pl.BlockSpec((d, n), const, pipeline_mode=pl.Buffered(1)),
                  pl.BlockSpec((1, rq), const), pl.BlockSpec((1, rkv), const),
                  pl.BlockSpec((tm, LANE), rows), pl.BlockSpec((tm, LANE), rows)]
                 + [pl.BlockSpec(memory_space=pl.ANY)] * n_alias,
        out_specs=[pl.BlockSpec((tm, rq), rows), pl.BlockSpec((tm, rkv), rows), pl.BlockSpec((tm, LANE), rows),
                   _state_spec(tm, g.s, nct, layer, g.s, rkv, False, colb),
                   _state_spec(tm, g.s, nct, layer, g.s, MLA_ROPE, True, colb)],
        out_shape=[jax.ShapeDtypeStruct((m, rq), BF16), jax.ShapeDtypeStruct((m, rkv), F32),
                   jax.ShapeDtypeStruct((m, LANE), BF16),
                   jax.ShapeDtypeStruct((g.b, n_layers, g.s, rkv), F32),
                   jax.ShapeDtypeStruct((g.b, n_layers, MLA_ROPE, g.s), F32)],
        input_output_aliases={6 + a: 3 + a for a in range(n_alias)},
        compiler_params=_params(("arbitrary",), d * n * 2 + 2 * tm * (d * 2 + n * 8) + tm * n * 8 + 2**23),
        name="mla_down",
    )(h, w_cat, _row_vec(q_norm), _row_vec(kv_norm), cos_t, sin_t, *(states or ()))


def _mla_q_body(cq_ref, w_ref, cos_ref, sin_ref, q_ref, *, heads, scale, seq):
    wins = _row_windows(cq_ref.shape[0], seq)
    accs = [_dot(cq_ref[w, :], w_ref[...]) for w in wins]
    for w, a in zip(wins, accs):
        cos, sin = cos_ref[w, :], sin_ref[w, :]
        for h in range(heads):
            lo = 2 * LANE * h
            q_ref[w, lo:lo + LANE] = (a[:, lo:lo + LANE] * scale).astype(BF16)
            q_ref[w, lo + LANE:lo + 2 * LANE] = (_rope(a[:, lo + LANE:lo + 2 * LANE], cos, sin) * scale).astype(BF16)


def _mla_q_call(cq, w_uq_pad, cos_t, sin_t, heads, g):
    m, rq = cq.shape
    n = w_uq_pad.shape[1]
    tm = _tile(math.gcd(g.mp, g.ts), 512)
    const = lambda i: (0, 0)
    rows = lambda i: (i, 0)
    scale = (MLA_NOPE + MLA_ROPE) ** -0.5 * LOG2E
    return pl.pallas_call(
        functools.partial(_mla_q_body, heads=heads, scale=scale, seq=g.s),
        grid=(m // tm,),
        in_specs=[pl.BlockSpec((tm, rq), rows),
                  pl.BlockSpec((rq, n), const, pipeline_mode=pl.Buffered(1)),
                  pl.BlockSpec((tm, LANE), rows), pl.BlockSpec((tm, LANE), rows)],
        out_specs=pl.BlockSpec((tm, n), rows),
        out_shape=jax.ShapeDtypeStruct((m, n), BF16),
        compiler_params=_params(("arbitrary",), rq * n * 2 + 2 * tm * (rq * 2 + n * 2) + 4 * tm * n * 4 + 2**23),
        name="mla_q_up",
    )(cq, w_uq_pad, cos_t, sin_t)


def _kv_up_body(x_ref, kpe_ref, w_ref, kcat_ref, vt_ref, *, heads):
    wins = _row_windows(x_ref.shape[0], LANE)
    accs = [_dot(x_ref[w, :].astype(BF16), w_ref[...]) for w in wins]
    for w, a in zip(wins, accs):
        kpe = kpe_ref[w, :]
        for h in range(heads):
            lo = 2 * LANE * h
            kcat_ref[w, lo:lo + LANE] = a[:, lo:lo + LANE].astype(BF16)
            kcat_ref[w, lo + LANE:lo + 2 * LANE] = kpe
            vt_ref[h * LANE:(h + 1) * LANE, w] = a[:, lo + LANE:lo + 2 * LANE].T.astype(BF16)


def _kv_up_call(x, kpe16, w_all, j, heads, name):
    m, k = x.shape
    n = w_all.shape[2]
    tm = _tile(m, 512)
    rows = lambda i: (i, 0)
    return pl.pallas_call(
        functools.partial(_kv_up_body, heads=heads),
        grid=(m // tm,),
        in_specs=[pl.BlockSpec((tm, k), rows), pl.BlockSpec((tm, LANE), rows),
                  pl.BlockSpec((None, k, n), lambda i: (j, 0, 0), pipeline_mode=pl.Buffered(1))],
        out_specs=[pl.BlockSpec((tm, 2 * heads * LANE), rows), pl.BlockSpec((heads * LANE, tm), lambda i: (0, i))],
        out_shape=[jax.ShapeDtypeStruct((m, 2 * heads * LANE), BF16), jax.ShapeDtypeStruct((heads * LANE, m), BF16)],
        compiler_params=_params(("arbitrary",), k * n * 2 + 2 * tm * (k * 4 + n * 3) + tm * n * 4 + 2**23),
        name=name,
    )(x, kpe16, w_all)


def _row_fold(x, op):
    out = x[:SUBLANE]
    for r in range(1, x.shape[0] // SUBLANE):
        out = op(out, x[r * SUBLANE:(r + 1) * SUBLANE])
    return out


def _chunks(segments):
    out = []
    for k_ref, vt_ref in segments:
        n = k_ref.shape[0]
        assert n % LANE == 0
        out += [(k_ref, vt_ref, r0, min(KV_CHUNK, n - r0)) for r0 in range(0, n, KV_CHUNK)]
    return out


def _scores_pass(qt, segments, s_ref):
    m = None
    off = 0
    for k_ref, _, r0, n in _chunks(segments):
        s = _dot(k_ref[r0:r0 + n, :], qt)
        s_ref[off:off + n, :] = s
        cm = _row_fold(s, jnp.maximum)
        m = cm if m is None else jnp.maximum(m, cm)
        off += n
    return jnp.max(m, axis=0, keepdims=True)


def _values_pass(segments, s_ref, m):
    acc = l = None
    off = 0
    for _, vt_ref, r0, n in _chunks(segments):
        e = jnp.exp2(s_ref[off:off + n, :] - m)
        cl = _row_fold(e, jnp.add)
        pv = _dot(vt_ref[:, r0:r0 + n], e.astype(BF16))
        l = cl if l is None else l + cl
        acc = pv if acc is None else acc + pv
        off += n
    return acc * (1.0 / jnp.sum(l, axis=0, keepdims=True))


def _pipelined_tiles(n_tiles, scores_of, values_of, s_a, s_b):
    if n_tiles == 1:
        values_of(0, s_a, scores_of(0, s_a))
        return
    assert n_tiles % 2 == 0

    def body(j, m_a):
        t = 2 * j
        m_b = scores_of(t + 1, s_b)
        values_of(t, s_a, m_a)
        m_next = scores_of(jnp.minimum(t + 2, n_tiles - 1), s_a)
        values_of(t + 1, s_b, m_b)
        return m_next

    lax.fori_loop(0, n_tiles // 2, body, scores_of(0, s_a))


def _tile_rows(t, n):
    return pl.ds(t * n, n) if isinstance(t, int) else pl.ds(pl.multiple_of(t * n, n), n)


def _transpose_bf16(x):
    return x.astype(F32).T.astype(BF16)


def _pipelined_static(tiles, scores_of, values_of, s_a, s_b):
    m_prev = None
    for idx, tile in enumerate(tiles):
        m_cur = scores_of(tile, (s_a, s_b)[idx % 2])
        if idx:
            values_of(tiles[idx - 1], (s_a, s_b)[(idx - 1) % 2], m_prev)
        m_prev = m_cur
    values_of(tiles[-1], (s_a, s_b)[(len(tiles) - 1) % 2], m_prev)


def _mla_attn_ctx_body(q_ref, k_ref, vt_ref, o_ref, s_a, s_b, *, heads):
    nq = s_a.shape[1]
    seg = lambda h: [(k_ref.at[:, 2 * LANE * h:2 * LANE * (h + 1)], vt_ref.at[LANE * h:LANE * (h + 1), :])]

    def scores_of(tile, s_ref):
        h, r0 = tile
        return _scores_pass(_transpose_bf16(q_ref[r0:r0 + nq, 2 * LANE * h:2 * LANE * (h + 1)]), seg(h), s_ref)

    def values_of(tile, s_ref, m):
        h, r0 = tile
        o_ref[r0:r0 + nq, LANE * h:LANE * (h + 1)] = _values_pass(seg(h), s_ref, m).T.astype(BF16)

    tiles = [(h, r0) for h in range(heads) for r0 in range(0, q_ref.shape[0], nq)]
    _pipelined_static(tiles, scores_of, values_of, s_a, s_b)


def _mla_attn_lat_body(q_ref, kc_ref, vc_ref, kn_ref, vn_ref, o_alias, o_ref, s_a, s_b):
    del o_alias
    segs = [(kc_ref, vc_ref), (kn_ref, vn_ref)]

    def scores_of(t, s_ref):
        return _scores_pass(_transpose_bf16(q_ref[_tile_rows(t, Q_TILE), :]), segs, s_ref)

    def values_of(t, s_ref, m):
        o_ref[_tile_rows(t, Q_TILE), :] = _values_pass(segs, s_ref, m).T.astype(BF16)

    _pipelined_tiles(q_ref.shape[0] // Q_TILE, scores_of, values_of, s_a, s_b)


def _mla_attn(q, kcat, vt, kcat_c, vt_c, heads, g):
    m = q.shape[0]
    hn = heads * LANE
    past = kcat_c.shape[0] // g.bs
    o = pl.pallas_call(
        functools.partial(_mla_attn_ctx_body, heads=heads),
        grid=(g.b,),
        in_specs=[pl.BlockSpec((g.s, 2 * hn), lambda b: (b, 0)), pl.BlockSpec((g.s, 2 * hn), lambda b: (b, 0)),
                  pl.BlockSpec((hn, g.s), lambda b: (0, b))],
        out_specs=pl.BlockSpec((g.s, hn), lambda b: (b, 0)),
        out_shape=jax.ShapeDtypeStruct((m, hn), BF16),
        scratch_shapes=[pltpu.VMEM((g.s, min(Q_TILE, g.s)), F32)] * 2,
        compiler_params=_params(("arbitrary",), 2 * g.s * hn * 12 + 2**24),
        name="mla_attn_ctx",
    )(q, kcat, vt)
    kb0 = g.mp // g.ts
    return pl.pallas_call(
        _mla_attn_lat_body,
        grid=(g.bs, heads),
        in_specs=[pl.BlockSpec((g.ts, 2 * LANE), lambda b, h: (kb0 + b, h)),
                  pl.BlockSpec((past, 2 * LANE), lambda b, h: (b, h)),
                  pl.BlockSpec((LANE, past), lambda b, h: (h, b)),
                  pl.BlockSpec((g.ts, 2 * LANE), lambda b, h: (kb0 + b, h)),
                  pl.BlockSpec((LANE, g.ts), lambda b, h: (h, kb0 + b)),
                  pl.BlockSpec(memory_space=pl.ANY)],
        out_specs=pl.BlockSpec((g.ts, LANE), lambda b, h: (kb0 + b, h)),
        out_shape=jax.ShapeDtypeStruct((m, hn), BF16),
        input_output_aliases={5: 0},
        scratch_shapes=[pltpu.VMEM((past + g.ts, Q_TILE), F32)] * 2,
        compiler_params=_params(("arbitrary", "arbitrary"), 2**25),
        name="mla_attn_lat",
    )(q, kcat_c, vt_c, kcat, vt, o)


def _diff_lambda(lam_ref, lam_init):
    lv = lam_ref[...]
    s1 = jnp.sum(lv[0:1] * lv[1:2], axis=-1, keepdims=True)
    s2 = jnp.sum(lv[2:3] * lv[3:4], axis=-1, keepdims=True)
    return jnp.exp(s1) - jnp.exp(s2) + lam_init


def _diff_queries_t(q):
    qt = q.astype(F32).T
    row = lax.broadcasted_iota(jnp.int32, qt.shape, 0)
    return jnp.concatenate([jnp.where(row < DIFF_DK, qt, 0.0), jnp.where(row < DIFF_DK, 0.0, qt)],
                           axis=1).astype(BF16)


def _diff_combine(ot, lam, subln_col, lam_init):
    n = ot.shape[1] // 2
    dt = ot[:, :n] - lam * ot[:, n:]
    dt = dt * lax.rsqrt(jnp.mean(dt * dt, axis=0, keepdims=True) + NORM_EPS) * subln_col * (1.0 - lam_init)
    return dt.T.astype(BF16)


def _diff_attn_ctx_body(q_ref, k_ref, vt_ref, lam_ref, sub_ref, o_ref, s_a, s_b, *, heads, lam_init):
    lam = _diff_lambda(lam_ref, lam_init)
    nq = s_a.shape[1] // 2
    cols = lambda h: slice(LANE * h, LANE * (h + 1))
    seg = lambda h: [(k_ref.at[:, cols(h)], vt_ref.at[cols(h), :])]

    def scores_of(tile, s_ref):
        h, r0 = tile
        return _scores_pass(_diff_queries_t(q_ref[r0:r0 + nq, cols(h)]), seg(h), s_ref)

    def values_of(tile, s_ref, m):
        h, r0 = tile
        o_ref[r0:r0 + nq, cols(h)] = _diff_combine(_values_pass(seg(h), s_ref, m), lam, sub_ref[...], lam_init)

    tiles = [(h, r0) for h in range(heads) for r0 in range(0, q_ref.shape[0], nq)]
    _pipelined_static(tiles, scores_of, values_of, s_a, s_b)


def _diff_attn_lat_body(q_ref, kc_ref, vc_ref, kn_ref, vn_ref, lam_ref, sub_ref, o_alias, o_ref, s_a, s_b, *,
                        lam_init):
    del o_alias
    lam = _diff_lambda(lam_ref, lam_init)
    segs = [(kc_ref, vc_ref), (kn_ref, vn_ref)]
    nq = Q_TILE // 2

    def scores_of(t, s_ref):
        return _scores_pass(_diff_queries_t(q_ref[_tile_rows(t, nq), :]), segs, s_ref)

    def values_of(t, s_ref, m):
        o_ref[_tile_rows(t, nq), :] = _diff_combine(_values_pass(segs, s_ref, m), lam, sub_ref[...], lam_init)

    _pipelined_tiles(q_ref.shape[0] // nq, scores_of, values_of, s_a, s_b)


def _diff_attn(q, k, vt, k_c, vt_c, lam_vecs, subln, lam_init, heads, g):
    m = q.shape[0]
    hn = heads * LANE
    past = k_c.shape[0] // g.bs
    sub = subln.reshape(-1, 1)
    blk = lambda b: (b, 0)
    o = pl.pallas_call(
        functools.partial(_diff_attn_ctx_body, heads=heads, lam_init=lam_init),
        grid=(g.b,),
        in_specs=[pl.BlockSpec((g.s, hn), blk), pl.BlockSpec((g.s, hn), blk), pl.BlockSpec((hn, g.s), lambda b: (0, b)),
                  pl.BlockSpec((4, DIFF_DK), lambda b: (0, 0)), pl.BlockSpec((DIFF_DV, 1), lambda b: (0, 0))],
        out_specs=pl.BlockSpec((g.s, hn), blk),
        out_shape=jax.ShapeDtypeStruct((m, hn), BF16),
        scratch_shapes=[pltpu.VMEM((g.s, min(Q_TILE, 2 * g.s)), F32)] * 2,
        compiler_params=_params(("arbitrary",), 2 * g.s * hn * 12 + 2**24),
        name="diff_attn_ctx",
    )(q, k, vt, lam_vecs, sub)
    kb0 = g.mp // g.ts
    return pl.pallas_call(
        functools.partial(_diff_attn_lat_body, lam_init=lam_init),
        grid=(g.bs, heads),
        in_specs=[pl.BlockSpec((g.ts, LANE), lambda b, h: (kb0 + b, h)),
                  pl.BlockSpec((past, LANE), lambda b, h: (b, h)),
                  pl.BlockSpec((LANE, past), lambda b, h: (h, b)),
                  pl.BlockSpec((g.ts, LANE), lambda b, h: (kb0 + b, h)),
                  pl.BlockSpec((LANE, g.ts), lambda b, h: (h, kb0 + b)),
                  pl.BlockSpec((4, DIFF_DK), lambda b, h: (0, 0)),
                  pl.BlockSpec((DIFF_DV, 1), lambda b, h: (0, 0)),
                  pl.BlockSpec(memory_space=pl.ANY)],
        out_specs=pl.BlockSpec((g.ts, LANE), lambda b, h: (kb0 + b, h)),
        out_shape=jax.ShapeDtypeStruct((m, hn), BF16),
        input_output_aliases={7: 0},
        scratch_shapes=[pltpu.VMEM((past + g.ts, Q_TILE), F32)] * 2,
        compiler_params=_params(("arbitrary", "arbitrary"), 2**25),
        name="diff_attn_lat",
    )(q, k_c, vt_c, k, vt, lam_vecs, sub, o)


def _proj_body(h_ref, w_ref, cos_ref, sin_ref, *refs, rope, scale, kinds, seq, n_ctx_tiles, n_alias):
    out_refs = refs[n_alias:]
    i = pl.program_id(1)
    wins = _row_windows(h_ref.shape[0], max(seq, LANE))
    accs = [_dot(h_ref[w, :], w_ref[...]) for w in wins]
    for w, a in zip(wins, accs):
        if rope:
            reps = a.shape[1] // LANE
            a = _rope(a, _lane_tile(cos_ref[w, :], reps), _lane_tile(sin_ref[w, :], reps))
        if scale != 1.0:
            a = a * scale
        at = a.T if any(k in ("cols", "state_cols") for k in kinds) else None
        for o_ref, kind in zip(out_refs, kinds):
            if kind == "rows":
                o_ref[w, :] = a.astype(o_ref.dtype)
            elif kind == "cols":
                o_ref[:, w] = at.astype(o_ref.dtype)
            else:
                @pl.when(i < n_ctx_tiles)
                def _(o_ref=o_ref, kind=kind, a=a, at=at, w=w):
                    _store_state(o_ref, w, seq, at if kind == "state_cols" else a, kind == "state_cols")


def _proj_call(h, w_all, layer, col0, ncols, cos_t, sin_t, rope, scale, outs, st_layer, st_layers, state, g, name):
    m, d = h.shape
    tn = _tile(ncols, 1024, LANE)
    tm = _tile(math.gcd(g.mp, g.ts), 512, LANE)
    assert tm % g.s == 0
    nct = g.mp // tm
    nb0 = col0 // tn
    colb = lambda j, i: (i, j)
    specs, shapes = [], []
    for kind, dt in outs:
        if kind == "rows":
            specs.append(pl.BlockSpec((tm, tn), lambda j, i: (i, j)))
            shapes.append(jax.ShapeDtypeStruct((m, ncols), dt))
        elif kind == "cols":
            specs.append(pl.BlockSpec((tn, tm), lambda j, i: (j, i)))
            shapes.append(jax.ShapeDtypeStruct((ncols, m), dt))
        else:
            tr = kind == "state_cols"
            specs.append(_state_spec(tm, g.s, nct, st_layer, g.s, tn, tr, colb))
            shapes.append(jax.ShapeDtypeStruct((g.b, st_layers, ncols, g.s) if tr else (g.b, st_layers, g.s, ncols), dt))
    st_idx = [k for k, (kind, _) in enumerate(outs) if kind.startswith("state")]
    n_alias = len(st_idx) if state is not None else 0
    return pl.pallas_call(
        functools.partial(_proj_body, rope=rope, scale=scale, kinds=tuple(k for k, _ in outs), seq=g.s,
                          n_ctx_tiles=nct, n_alias=n_alias),
        grid=(ncols // tn, m // tm),
        in_specs=[pl.BlockSpec((tm, d), lambda j, i: (i, 0)),
                  pl.BlockSpec((None, d, tn), lambda j, i: (layer, 0, nb0 + j)),
                  pl.BlockSpec((tm, LANE), lambda j, i: (i, 0)), pl.BlockSpec((tm, LANE), lambda j, i: (i, 0))]
                 + [pl.BlockSpec(memory_space=pl.ANY)] * n_alias,
        out_specs=specs,
        out_shape=shapes,
        input_output_aliases={4 + a: st_idx[a] for a in range(n_alias)},
        compiler_params=_params(("arbitrary", "arbitrary"),
                                d * tn * 4 + 2 * tm * (d * 2 + tn * 6) + 4 * tm * tn * 4 + 2**23),
        name=name,
    )(h, w_all, cos_t, sin_t, *([state] if n_alias else []))


def _out_ln_body(o_ref, w_ref, x_ref, gate_ref, g_ref, b_ref, sc_ref, sh_ref, xo_ref, ho_ref, *, alpha):
    wins = _row_windows(o_ref.shape[0], 16)
    accs = [_dot(o_ref[w, :], w_ref[...]) for w in wins]
    for w, y in zip(wins, accs):
        xn = _layer_norm(alpha * x_ref[w, :] + gate_ref[...] * y, g_ref[...], b_ref[...])
        xo_ref[w, :] = xn
        ho_ref[w, :] = (xn * (1.0 + sc_ref[...]) + sh_ref[...]).astype(BF16)


def _out_ln_call(o, w_o_all, layer, x, mod, ln_g, ln_b, alpha, g):
    m, k = o.shape
    d = w_o_all.shape[2]
    tm = _tile(math.gcd(g.mp, g.ts), 512)
    row = lambda i: _mod_row(i, tm, g.mp, g.ts)
    const = lambda i: (0, 0)
    rows = lambda i: (i, 0)
    return pl.pallas_call(
        functools.partial(_out_ln_body, alpha=alpha),
        grid=(m // tm,),
        in_specs=[pl.BlockSpec((tm, k), rows),
                  pl.BlockSpec((None, k, d), lambda i: (layer, 0, 0), pipeline_mode=pl.Buffered(1)),
                  pl.BlockSpec((tm, d), rows), _mod_spec(2, d, row),
                  pl.BlockSpec((1, d), const), pl.BlockSpec((1, d), const),
                  _mod_spec(4, d, row), _mod_spec(3, d, row)],
        out_specs=[pl.BlockSpec((tm, d), rows), pl.BlockSpec((tm, d), rows)],
        out_shape=[jax.ShapeDtypeStruct((m, d), F32), jax.ShapeDtypeStruct((m, d), BF16)],
        compiler_params=_params(("arbitrary",), k * d * 2 + 2 * tm * (k * 2 + d * 10) + 4 * tm * d * 4 + 2**23),
        name="out_proj_ln",
    )(o, w_o_all, x, mod, _row_vec(ln_g), _row_vec(ln_b), mod, mod)


def _lane_cumsum(x, tri):
    c = tri.shape[0]
    carry = jnp.zeros((x.shape[0], 1), F32)
    parts = []
    for j in range(x.shape[1] // c):
        xc = x[:, j * c:(j + 1) * c]
        parts.append(_dot(xc.astype(BF16), tri) + carry)
        carry = carry + jnp.sum(xc, axis=1, keepdims=True)
    return parts[0] if len(parts) == 1 else jnp.concatenate(parts, axis=1)


def _route_body(h_ref, wr_ref, slot_ref, gate_ref, *, groups, t, cap):
    logits = _dot_nt(wr_ref[...].astype(BF16), h_ref[...])
    ex = jnp.exp(logits - jnp.max(logits, axis=0, keepdims=True))
    aff = ex / jnp.sum(ex, axis=0, keepdims=True)
    nbits = jnp.finfo(aff.dtype).bits
    keys = lax.bitcast_convert_type(aff, jnp.dtype(f"int{nbits}"))
    c = min(t, 2 * LANE)
    tri = jnp.where(lax.broadcasted_iota(jnp.int32, (c, c), 0) <= lax.broadcasted_iota(jnp.int32, (c, c), 1),
                    1.0, 0.0).astype(BF16)
    capf = float(cap)
    for gi in range(groups):
        k = keys[:, gi * t:(gi + 1) * t]
        thr = jnp.zeros((k.shape[0], 1), keys.dtype)
        for bit in range(nbits - 2, -1, -1):
            cand = thr | (1 << bit)
            cnt = jnp.sum(jnp.where(k >= cand, 1.0, 0.0), axis=1, keepdims=True)
            thr = jnp.where(cnt >= capf, cand, thr)
        gt = jnp.where(k > thr, 1.0, 0.0)
        eq = jnp.where(k == thr, 1.0, 0.0)
        need = capf - jnp.sum(gt, axis=1, keepdims=True)
        eq_before = _lane_cumsum(eq, tri) - eq
        sel = gt + eq * jnp.where(eq_before < need, 1.0, 0.0)
        pos = _lane_cumsum(sel, tri) - 1.0
        slot_ref[gi] = jnp.where(sel > 0.5, pos, -1.0).astype(jnp.int32)
        gate_ref[gi] = aff[:, gi * t:(gi + 1) * t]


def _route_call(h2, w_router_t, row0, nreq, t, cap, groups, name):
    e, d = w_router_t.shape
    rb0 = row0 // (groups * t)
    return pl.pallas_call(
        functools.partial(_route_body, groups=groups, t=t, cap=cap),
        grid=(nreq // groups,),
        in_specs=[pl.BlockSpec((groups * t, d), lambda i: (rb0 + i, 0)), pl.BlockSpec((e, d), lambda i: (0, 0))],
        out_specs=[pl.BlockSpec((groups, e, t), lambda i: (i, 0, 0)), pl.BlockSpec((groups, e, t), lambda i: (i, 0, 0))],
        out_shape=[jax.ShapeDtypeStruct((nreq, e, t), jnp.int32), jax.ShapeDtypeStruct((nreq, e, t), F32)],
        compiler_params=_params(("parallel",), 2 * groups * t * d * 2 + 2**24),
        name=name,
    )(h2, w_router_t)


def _one_hot_rows(slot_row, cap):
    r = lax.broadcasted_iota(jnp.int32, (cap, slot_row.shape[1]), 0)
    return slot_row == r


def _gather_ctx_body(h_ref, slot_ref, gate_ref, xs_ref, gs_ref, *, cap):
    ne = slot_ref.shape[0]
    masks = [_one_hot_rows(slot_ref[e:e + 1, :], cap) for e in range(ne)]
    p = jnp.concatenate([jnp.where(mk, 1.0, 0.0) for mk in masks], axis=0).astype(BF16)
    xs = _dot(p, h_ref[...]).astype(BF16)
    for e in range(ne):
        xs_ref[e] = xs[e * cap:(e + 1) * cap]
        gs_ref[e] = jnp.sum(jnp.where(masks[e], gate_ref[e:e + 1, :], 0.0), axis=1, keepdims=True)


def _gather_lat_body(h_ref, slot_ref, gate_ref, xs_alias, gs_alias, xs_ref, gs_ref, *, cap):
    del xs_alias, gs_alias
    e = pl.program_id(1)
    mk = _one_hot_rows(slot_ref[pl.ds(e, 1), :], cap)
    xs_ref[...] = _dot(jnp.where(mk, 1.0, 0.0).astype(BF16), h_ref[...]).astype(BF16)
    gs_ref[...] = jnp.sum(jnp.where(mk, gate_ref[pl.ds(e, 1), :], 0.0), axis=1, keepdims=True)


def _gather(h2, slot_p, gate_p, slot_s, gate_s, g):
    d = h2.shape[1]
    ne = slot_p.shape[1]
    rows_e = g.b * g.cap_p + g.bs * g.cap_s
    shapes = [jax.ShapeDtypeStruct((ne, rows_e, d), BF16), jax.ShapeDtypeStruct((ne, rows_e, 1), F32)]
    xs, gs = pl.pallas_call(
        functools.partial(_gather_ctx_body, cap=g.cap_p),
        grid=(g.b,),
        in_specs=[pl.BlockSpec((g.s, d), lambda b: (b, 0)),
                  pl.BlockSpec((None, ne, g.s), lambda b: (b, 0, 0)), pl.BlockSpec((None, ne, g.s), lambda b: (b, 0, 0))],
        out_specs=[pl.BlockSpec((ne, g.cap_p, d), lambda b: (0, b, 0)), pl.BlockSpec((ne, g.cap_p, 1), lambda b: (0, b, 0))],
        out_shape=shapes,
        compiler_params=_params(("parallel",), 2 * ne * g.cap_p * (d * 8 + g.s * 8) + 2**24),
        name="moe_gather_ctx",
    )(h2, slot_p, gate_p)
    blk0 = (g.b * g.cap_p) // g.cap_s
    hb0 = g.mp // g.ts
    return pl.pallas_call(
        functools.partial(_gather_lat_body, cap=g.cap_s),
        grid=(g.bs, ne),
        in_specs=[pl.BlockSpec((g.ts, d), lambda b, e: (hb0 + b, 0)),
                  pl.BlockSpec((None, ne, g.ts), lambda b, e: (b, 0, 0)), pl.BlockSpec((None, ne, g.ts), lambda b, e: (b, 0, 0)),
                  pl.BlockSpec(memory_space=pl.ANY), pl.BlockSpec(memory_space=pl.ANY)],
        out_specs=[pl.BlockSpec((None, g.cap_s, d), lambda b, e: (e, blk0 + b, 0)),
                   pl.BlockSpec((None, g.cap_s, 1), lambda b, e: (e, blk0 + b, 0))],
        out_shape=shapes,
        input_output_aliases={3: 0, 4: 1},
        compiler_params=_params(("parallel", "parallel"), 2 * g.ts * d * 2 + g.cap_s * (g.ts * 12 + d * 12) + 2**24),
        name="moe_gather_lat",
    )(h2, slot_s, gate_s, xs, gs)


def _ffn1_body(x_ref, wg_ref, wu_ref, o_ref):
    x = x_ref[...]
    a = _dot(x, wg_ref[...].astype(BF16))
    u = _dot(x, wu_ref[...].astype(BF16))
    o_ref[...] = (a * _sigmoid(a) * u).astype(BF16)


def _ffn1_call(xs, w_gate, w_up, layer):
    ne, rows, d = xs.shape
    f = w_gate.shape[3]
    tn = _tile(f, 256, LANE)
    wspec = pl.BlockSpec((None, None, d, tn), lambda e, n: (layer, e, 0, n))
    return pl.pallas_call(
        _ffn1_body,
        grid=(ne, f // tn),
        in_specs=[pl.BlockSpec((None, rows, d), lambda e, n: (e, 0, 0)), wspec, wspec],
        out_specs=pl.BlockSpec((None, rows, tn), lambda e, n: (e, 0, n)),
        out_shape=jax.ShapeDtypeStruct((ne, rows, f), BF16),
        compiler_params=_params(("parallel", "parallel"), 2 * rows * d * 2 + 4 * d * tn * 4 + 2 * d * tn * 2 + 5 * rows * tn * 4 + 2**23),
        name="moe_ffn_gate_up",
    )(xs, w_gate, w_up)


def _ffn2_body(h_ref, w_ref, gs_ref, o_ref):
    o_ref[...] = (_dot(h_ref[...], w_ref[...].astype(BF16)) * gs_ref[...]).astype(BF16)


def _ffn2_call(hm, w_down, gs, layer):
    ne, rows, f = hm.shape
    d = w_down.shape[3]
    tn = _tile(d, 256, LANE)
    return pl.pallas_call(
        _ffn2_body,
        grid=(ne, d // tn),
        in_specs=[pl.BlockSpec((None, rows, f), lambda e, n: (e, 0, 0)),
                  pl.BlockSpec((None, None, f, tn), lambda e, n: (layer, e, 0, n)),
                  pl.BlockSpec((None, rows, 1), lambda e, n: (e, 0, 0))],
        out_specs=pl.BlockSpec((None, rows, tn), lambda e, n: (e, 0, n)),
        out_shape=jax.ShapeDtypeStruct((ne, rows, d), BF16),
        compiler_params=_params(("parallel", "parallel"), 2 * rows * f * 2 + 2 * f * tn * 4 + f * tn * 2 + 3 * rows * tn * 4 + 2 * rows * LANE * 4 + 2**23),
        name="moe_ffn_down",
    )(hm, w_down, gs)


def _combine(y_ref, slot, cap):
    ne = y_ref.shape[0]
    p = jnp.concatenate([jnp.where(_one_hot_rows(slot[e:e + 1, :], cap), 1.0, 0.0) for e in range(ne)], axis=0)
    y = y_ref[...].reshape(ne * cap, y_ref.shape[2])
    return _dot_tn(p.astype(BF16), y)


def _combine_ln_body(*refs, cap, alpha, n_alias, emit_h):
    y_ref, slot_ref, x_ref, gate_ref, g_ref, b_ref = refs[:6]
    rest = refs[6:]
    if emit_h:
        sc_ref, sh_ref = rest[:2]
        rest = rest[2:]
    outs = rest[n_alias:]
    moe = _combine(y_ref, slot_ref[...], cap)
    xn = _layer_norm(alpha * x_ref[...] + gate_ref[...] * moe, g_ref[...], b_ref[...])
    outs[0][...] = xn
    if emit_h:
        outs[1][...] = (xn * (1.0 + sc_ref[...]) + sh_ref[...]).astype(BF16)


def _combine_ln(y, slot_p, slot_s, x, mod, mod_next, ln_g, ln_b, alpha, g):
    ne, rows_e, d = y.shape
    m = x.shape[0]
    emit_h = mod_next is not None
    lg, lb = _row_vec(ln_g), _row_vec(ln_b)
    shapes = [jax.ShapeDtypeStruct((m, d), F32), jax.ShapeDtypeStruct((m, d), BF16)] if emit_h else \
        [jax.ShapeDtypeStruct((g.mp, d), F32)]
    n_out = len(shapes)

    def specs(row):
        sp = [pl.BlockSpec((1, d), lambda *a: (0, 0)), pl.BlockSpec((1, d), lambda *a: (0, 0))]
        md = [_mod_spec(5, d, row)]
        nx = [_mod_spec(1, d, row), _mod_spec(0, d, row)] if emit_h else []
        return md, sp, nx

    md, sp, nx = specs(lambda b: 0)
    outs = pl.pallas_call(
        functools.partial(_combine_ln_body, cap=g.cap_p, alpha=alpha, n_alias=0, emit_h=emit_h),
        grid=(g.b,),
        in_specs=[pl.BlockSpec((ne, g.cap_p, d), lambda b: (0, b, 0)),
                  pl.BlockSpec((None, ne, g.s), lambda b: (b, 0, 0)),
                  pl.BlockSpec((g.s, d), lambda b: (b, 0))] + md + sp + nx,
        out_specs=[pl.BlockSpec((g.s, d), lambda b: (b, 0))] * n_out,
        out_shape=shapes,
        compiler_params=_params(("parallel",), 2 * ne * g.cap_p * (d * 2 + g.s * 8) + g.s * d * 40 + 2**24),
        name="moe_combine_ln_ctx",
    )(y, slot_p, x, mod, lg, lb, *([mod_next, mod_next] if emit_h else []))
    outs = list(outs) if isinstance(outs, (list, tuple)) else [outs]
    tt = _tile(g.ts, 256, LANE)
    nt = g.ts // tt
    blk0 = (g.b * g.cap_p) // g.cap_s
    xb0 = g.mp // tt
    md, sp, nx = specs(lambda b, i: 1 + b)
    n_in = 6 + len(nx)
    n_alias = n_out if emit_h else 0
    ob0 = xb0 if emit_h else 0
    outs_s = pl.pallas_call(
        functools.partial(_combine_ln_body, cap=g.cap_s, alpha=alpha, n_alias=n_alias, emit_h=emit_h),
        grid=(g.bs, nt),
        in_specs=[pl.BlockSpec((ne, g.cap_s, d), lambda b, i: (0, blk0 + b, 0), pipeline_mode=pl.Buffered(1)),
                  pl.BlockSpec((None, ne, tt), lambda b, i: (b, 0, i)),
                  pl.BlockSpec((tt, d), lambda b, i: (xb0 + b * nt + i, 0))] + md + sp + nx
                 + [pl.BlockSpec(memory_space=pl.ANY)] * n_alias,
        out_specs=[pl.BlockSpec((tt, d), lambda b, i: (ob0 + b * nt + i, 0))] * n_out,
        out_shape=shapes if emit_h else [jax.ShapeDtypeStruct((g.ms, d), F32)],
        input_output_aliases={n_in + j: j for j in range(n_alias)},
        compiler_params=_params(("parallel", "parallel"), ne * g.cap_s * (d * 2 + tt * 10) + tt * d * 48 + 2**24),
        name="moe_combine_ln_lat",
    )(y, slot_s, x, mod, lg, lb, *([mod_next, mod_next] if emit_h else []), *(outs if emit_h else []))
    outs_s = list(outs_s) if isinstance(outs_s, (list, tuple)) else [outs_s]
    if emit_h:
        return outs_s[0], outs_s[1]
    return outs[0], outs_s[0]


class _Geom:
    def __init__(self, b, s, bs, ts, ne):
        self.b, self.s, self.bs, self.ts = b, s, bs, ts
        self.mp, self.ms = b * s, bs * ts
        self.cap_p = CAP_FACTOR * s // ne
        self.cap_s = CAP_FACTOR * ts // ne
        assert ts % s == 0 and self.mp % ts == 0 and bs + 1 <= MOD_ROWS
        assert (b * self.cap_p) % self.cap_s == 0 and self.cap_p % 16 == 0


def _rope_tables(g):
    rows = g.ts // GRID_W
    row = jnp.repeat(jnp.arange(rows, dtype=F32), GRID_W)
    col = jnp.tile(jnp.arange(GRID_W, dtype=F32), rows)
    half = ROT_DIM // 2
    inv_freq = ROPE_BASE ** (-jnp.arange(0, half, 2, dtype=F32) / half)
    ang_r = row[:, None] * inv_freq
    ang_c = col[:, None] * inv_freq
    ang = jnp.concatenate([ang_r, ang_r, ang_c, ang_c] * (LANE // ROT_DIM), axis=-1)
    cos = jnp.concatenate([jnp.ones((g.mp, LANE), F32), jnp.tile(jnp.cos(ang), (g.bs, 1))], axis=0)
    sin = jnp.concatenate([jnp.zeros((g.mp, LANE), F32), jnp.tile(jnp.sin(ang), (g.bs, 1))], axis=0)
    return cos, sin


def _pad_heads(w, heads, width):
    k = w.shape[0]
    per = w.shape[1] // heads
    return jnp.pad(w.reshape(k, heads, per), ((0, 0), (0, 0), (0, width - per))).reshape(k, heads * width)


def kernel(x_prompt, x_sample, cache_mla_ckv, cache_mla_kpe, cache_diff_k, cache_diff_v, c, c_ctx, ada_w, ada_b, ln1_g, ln1_b, ln2_g, ln2_b, mla_w_dq, mla_q_norm, mla_w_uq, mla_w_dkv, mla_kv_norm, mla_w_ukv, mla_w_o, diff_w_qkv, diff_lambda_q1, diff_lambda_k1, diff_lambda_q2, diff_lambda_k2, diff_subln, diff_w_o, moe_w_router, moe_w_gate, moe_w_up, moe_w_down):
    b, s, d = x_prompt.shape
    bs, ts, _ = x_sample.shape
    depth = ada_w.shape[0]
    ne = moe_w_router.shape[-1]
    g = _Geom(b, s, bs, ts, ne)
    mla_heads = mla_w_o.shape[1] // MLA_V
    diff_heads = diff_w_o.shape[1] // DIFF_DV
    alpha = (2.0 * depth) ** 0.25
    past = cache_mla_ckv.shape[2]

    x = jnp.concatenate([x_prompt.reshape(g.mp, d), x_sample.reshape(g.ms, d)], axis=0)
    cv = jnp.concatenate([c_ctx[None, :], c, jnp.zeros((MOD_ROWS - 1 - bs, d), F32)], axis=0)
    mods = _mod_call(cv, ada_w, ada_b)
    mod_of = lambda l: mods[l].reshape(MOD_ROWS, 1, 6 * d)
    cos_t, sin_t = _rope_tables(g)
    h = _modcast_call(x, mod_of(0), g)

    n_mla, n_diff = mla_w_o.shape[0], diff_w_o.shape[0]
    mla_w_ukv, mla_w_o, diff_w_qkv, diff_w_o = (w.astype(BF16) for w in (mla_w_ukv, mla_w_o, diff_w_qkv, diff_w_o))
    st_mla = st_dk = st_dv = None
    for l in range(depth):
        mod = mod_of(l)
        j = l // 2
        if l % 2 == 0:
            w_down = jnp.pad(jnp.concatenate([mla_w_dq[j], mla_w_dkv[j]], axis=1),
                             ((0, 0), (0, LANE - MLA_ROPE))).astype(BF16)
            cq, ckv, kpe16, *st_mla = _mla_down_call(h, w_down, mla_q_norm[j], mla_kv_norm[j], cos_t, sin_t,
                                                     j, n_mla, st_mla, g)
            w_uq = _pad_heads(mla_w_uq[j], mla_heads, 2 * LANE).astype(BF16)
            q = _mla_q_call(cq, w_uq, cos_t, sin_t, mla_heads, g)
            kpe_c = jnp.pad(cache_mla_kpe[:, j], ((0, 0), (0, 0), (0, LANE - MLA_ROPE))).astype(BF16)
            kcat, vt = _kv_up_call(ckv, kpe16, mla_w_ukv, j, mla_heads, "mla_kv_up")
            kcat_c, vt_c = _kv_up_call(cache_mla_ckv[:, j].reshape(bs * past, -1), kpe_c.reshape(bs * past, LANE),
                                       mla_w_ukv, j, mla_heads, "mla_kv_up_cache")
            o = _mla_attn(q, kcat, vt, kcat_c, vt_c, mla_heads, g)
            w_o = mla_w_o
        else:
            lam_init = 0.8 - 0.6 * math.exp(-0.3 * l)
            nq = diff_heads * 2 * DIFF_DK
            nv = diff_heads * DIFF_DV
            q, = _proj_call(h, diff_w_qkv, j, 0, nq, cos_t, sin_t, True, DIFF_DK ** -0.5 * LOG2E,
                            [("rows", BF16)], j, n_diff, None, g, "diff_q")
            k16, st_dk = _proj_call(h, diff_w_qkv, j, nq, nq, cos_t, sin_t, True, 1.0,
                                    [("rows", BF16), ("state_cols", F32)], j, n_diff, st_dk, g, "diff_k")
            vt16, st_dv = _proj_call(h, diff_w_qkv, j, 2 * nq, nv, cos_t, sin_t, False, 1.0,
                                     [("cols", BF16), ("state_rows", F32)], j, n_diff, st_dv, g, "diff_v")
            k_c = cache_diff_k[:, j].reshape(bs * past, nq).astype(BF16)
            vt_c = cache_diff_v[:, j].reshape(bs * past, nv).T.astype(BF16)
            lam_vecs = jnp.stack([diff_lambda_q1[j], diff_lambda_k1[j], diff_lambda_q2[j], diff_lambda_k2[j]])
            o = _diff_attn(q, k16, vt16, k_c, vt_c, lam_vecs, diff_subln[j], lam_init, diff_heads, g)
            w_o = diff_w_o
        x, h2 = _out_ln_call(o, w_o, j, x, mod, ln1_g[l], ln1_b[l], alpha, g)

        w_rt = moe_w_router[l].T
        groups = max(1, min(g.b, g.ts // g.s))
        slot_p, gate_p = _route_call(h2, w_rt, 0, g.b, g.s, g.cap_p, groups, "moe_route_ctx")
        slot_s, gate_s = _route_call(h2, w_rt, g.mp, g.bs, g.ts, g.cap_s, 1, "moe_route_lat")
        xs, gs = _gather(h2, slot_p, gate_p, slot_s, gate_s, g)
        hm = _ffn1_call(xs, moe_w_gate, moe_w_up, l)
        y = _ffn2_call(hm, moe_w_down, gs, l)
        mod_next = mod_of(l + 1) if l + 1 < depth else None
        x, h = _combine_ln(y, slot_p, slot_s, x, mod, mod_next, ln2_g[l], ln2_b[l], alpha, g)

    st_ckv, st_kpe = st_mla
    return (x.reshape(b, s, d), h.reshape(bs, ts, d), st_ckv, jnp.swapaxes(st_kpe, 2, 3),
            jnp.moveaxis(st_dk.reshape(b, n_diff, diff_heads, 2, DIFF_DK, s), 5, 2),
            st_dv.reshape(b, n_diff, s, diff_heads, DIFF_DV))
```

```python
import functools
import math

import jax
import jax.numpy as jnp
from jax import lax
from jax.experimental import pallas as pl
from jax.experimental.pallas import tpu as pltpu

F32 = jnp.float32
BF16 = jnp.bfloat16

NORM_EPS = 1e-6
ROPE_BASE = 10000.0
GRID_W = 64
CAP_FACTOR = 2
MLA_NOPE, MLA_ROPE, MLA_V = 128, 64, 128
DIFF_DK, DIFF_DV = 64, 128
ROT_DIM = 64
LANE = 128
MOD_ROWS = 8
LOG2E = math.log2(math.e)
SUBLANE = 8
Q_TILE = 256
KV_CHUNK = 256
ROW_SPLIT = 2
TILE_UNROLL = 4
CTX_GROUP = 4
V7X_VMEM_BYTES = 64 * 2**20
VMEM_HEADROOM_BYTES = 6 * 2**20


def _params(semantics, vmem_bytes):
    limit = min(int(vmem_bytes), V7X_VMEM_BYTES - VMEM_HEADROOM_BYTES)
    return pltpu.CompilerParams(dimension_semantics=semantics, vmem_limit_bytes=limit)


def _tile(n, target, mult=16):
    t = min(n, target)
    while t > mult and (n % t or t % mult):
        t -= mult
    assert n % t == 0 and t % mult == 0, (n, target, mult)
    return t


def _dot(a, b):
    return jnp.dot(a, b, preferred_element_type=F32)


def _dot_nt(a, b):
    return lax.dot_general(a, b, (((1,), (1,)), ((), ())), preferred_element_type=F32)


def _dot_tn(a, b):
    return lax.dot_general(a, b, (((0,), (0,)), ((), ())), preferred_element_type=F32)


def _sigmoid(x):
    return 1.0 / (1.0 + jnp.exp(-x))


def _rms(x):
    return x * lax.rsqrt(jnp.mean(x * x, axis=-1, keepdims=True) + NORM_EPS)


def _layer_norm(z, g, b):
    mu = jnp.mean(z, axis=-1, keepdims=True)
    zc = z - mu
    var = jnp.mean(zc * zc, axis=-1, keepdims=True)
    return zc * lax.rsqrt(var + NORM_EPS) * g + b


def _rope(x, cos, sin):
    n = x.shape[-1]
    nxt = pltpu.roll(x, n - 16, 1)
    prv = pltpu.roll(x, 16, 1)
    lane = lax.broadcasted_iota(jnp.int32, x.shape, 1)
    rot = jnp.where((lane % 32) < 16, -nxt, prv)
    return x * cos + rot * sin


def _lane_tile(t, reps):
    return t if reps == 1 else jnp.concatenate([t] * reps, axis=-1)


def _mod_row(i, tm, mp, ts):
    r0 = i * tm
    return jnp.where(r0 < mp, 0, 1 + (r0 - mp) // ts)


def _mod_spec(which, d, row_fn):
    return pl.BlockSpec((None, 1, d), lambda *ids: (row_fn(*ids), 0, which))


def _row_vec(v):
    return v.reshape(1, -1)


def _mod_body(c_ref, w_ref, b_ref, o_ref):
    c = c_ref[...]
    s = (c * _sigmoid(c)).astype(BF16)
    o_ref[...] = _dot(s, w_ref[...].astype(BF16)) + b_ref[...]


def _mod_call(cv, ada_w, ada_b):
    nl, d, n6 = ada_w.shape
    tn = _tile(n6, 1024, LANE)
    return pl.pallas_call(
        _mod_body,
        grid=(nl, n6 // tn),
        in_specs=[pl.BlockSpec((MOD_ROWS, d), lambda l, n: (0, 0)),
                  pl.BlockSpec((None, d, tn), lambda l, n: (l, 0, n)),
                  pl.BlockSpec((None, 1, tn), lambda l, n: (l, 0, n))],
        out_specs=pl.BlockSpec((None, MOD_ROWS, tn), lambda l, n: (l, 0, n)),
        out_shape=jax.ShapeDtypeStruct((nl, MOD_ROWS, n6), F32),
        compiler_params=_params(("parallel", "parallel"), 4 * d * tn * 4 + 2**22),
        name="ada_mod",
    )(cv, ada_w, ada_b.reshape(nl, 1, n6))


def _modcast_body(x_ref, sc_ref, sh_ref, o_ref):
    o_ref[...] = (x_ref[...] * (1.0 + sc_ref[...]) + sh_ref[...]).astype(BF16)


def _modcast_call(x, mod, g):
    m, d = x.shape
    tm = _tile(math.gcd(g.mp, g.ts), 512)
    row = lambda i: _mod_row(i, tm, g.mp, g.ts)
    return pl.pallas_call(
        _modcast_body,
        grid=(m // tm,),
        in_specs=[pl.BlockSpec((tm, d), lambda i: (i, 0)), _mod_spec(1, d, row), _mod_spec(0, d, row)],
        out_specs=pl.BlockSpec((tm, d), lambda i: (i, 0)),
        out_shape=jax.ShapeDtypeStruct((m, d), BF16),
        compiler_params=_params(("parallel",), 2 * tm * d * 6 + 2**22),
        name="modulate_in",
    )(x, mod, mod)


def _row_windows(tm, unit):
    n = ROW_SPLIT if tm % (ROW_SPLIT * unit) == 0 else 1
    return [slice(r, r + tm // n) for r in range(0, tm, tm // n)]


def _store_state(st_ref, win, seq, value, transposed):
    for r in range((win.stop - win.start) // seq):
        tok = slice(r * seq, (r + 1) * seq)
        st_ref[win.start // seq + r] = value[:, tok] if transposed else value[tok]


def _state_spec(tm, seq, n_ctx_tiles, layer, rows, cols, transposed, col_block):
    shape = (tm // seq, None, cols, rows) if transposed else (tm // seq, None, rows, cols)

    def index(*ids):
        i, jn = col_block(*ids)
        tile = jnp.minimum(i, n_ctx_tiles - 1)
        return (tile, layer, jn, 0) if transposed else (tile, layer, 0, jn)

    return pl.BlockSpec(shape, index)


def _mla_down_body(h_ref, w_ref, qn_ref, kvn_ref, cos_ref, sin_ref, *refs, rq, rkv, seq, n_ctx_tiles, n_alias):
    cq_ref, ckv_ref, kpe16_ref, st_ckv_ref, st_kpe_ref = refs[n_alias:]
    i = pl.program_id(0)
    wins = _row_windows(h_ref.shape[0], seq)
    accs = [_dot(h_ref[w, :], w_ref[...]) for w in wins]
    for w, a in zip(wins, accs):
        cq_ref[w, :] = (_rms(a[:, :rq]) * qn_ref[...]).astype(BF16)
        ckv = _rms(a[:, rq:rq + rkv]) * kvn_ref[...]
        ckv_ref[w, :] = ckv
        kpe = _rope(a[:, rq + rkv:], cos_ref[w, :], sin_ref[w, :])
        kpe16_ref[w, :] = kpe.astype(BF16)

        @pl.when(i < n_ctx_tiles)
        def _():
            _store_state(st_ckv_ref, w, seq, ckv, False)
            _store_state(st_kpe_ref, w, seq, kpe.T[:MLA_ROPE], True)


def _mla_down_call(h, w_cat, q_norm, kv_norm, cos_t, sin_t, layer, n_layers, states, g):
    m, d = h.shape
    rq, rkv = q_norm.shape[0], kv_norm.shape[0]
    n = w_cat.shape[1]
    tm = _tile(math.gcd(g.mp, g.ts), 512)
    assert tm % g.s == 0
    nct = g.mp // tm
    const = lambda i: (0, 0)
    rows = lambda i: (i, 0)
    n_alias = 0 if states is None else 2
    colb = lambda i: (i, 0)
    return pl.pallas_call(
        functools.partial(_mla_down_body, rq=rq, rkv=rkv, seq=g.s, n_ctx_tiles=nct, n_alias=n_alias),
        grid=(m // tm,),
        in_specs=[pl.BlockSpec((tm, d), rows),
                  pl.BlockSpec((d, n), const, pipeline_mode=pl.Buffered(1)),
                  pl.BlockSpec((1, rq), const), pl.BlockSpec((1, rkv), const),
                  pl.BlockSpec((tm, LANE), rows), pl.BlockSpec((tm, LANE), rows)]
                 + [pl.BlockSpec(memory_space=pl.ANY)] * n_alias,
        out_specs=[pl.BlockSpec((tm, rq), rows), pl.BlockSpec((tm, rkv), rows), pl.BlockSpec((tm, LANE), rows),
                   _state_spec(tm, g.s, nct, layer, g.s, rkv, False, colb),
                   _state_spec(tm, g.s, nct, layer, g.s, MLA_ROPE, True, colb)],
        out_shape=[jax.ShapeDtypeStruct((m, rq), BF16), jax.ShapeDtypeStruct((m, rkv), F32),
                   jax.ShapeDtypeStruct((m, LANE), BF16),
                   jax.ShapeDtypeStruct((g.b, n_layers, g.s, rkv), F32),
                   jax.ShapeDtypeStruct((g.b, n_layers, MLA_ROPE, g.s), F32)],
        input_output_aliases={6 + a: 3 + a for a in range(n_alias)},
        compiler_params=_params(("arbitrary",), d * n * 2 + 2 * tm * (d * 2 + n * 8) + tm * n * 8 + 2**23),
        name="mla_down",
    )(h, w_cat, _row_vec(q_norm), _row_vec(kv_norm), cos_t, sin_t, *(states or ()))


def _store_col_tiles(o_ref, win, qt, value_t):
    for r in range((win.stop - win.start) // qt):
        o_ref[win.start // qt + r] = value_t[:, r * qt:(r + 1) * qt].astype(o_ref.dtype)


def _rope_t(x, cos, sin):
    q = ROT_DIM // 4
    rot = jnp.concatenate([-x[q:2 * q], x[:q], -x[3 * q:], x[2 * q:3 * q]], axis=0)
    return x * cos + rot * sin


def _mla_q_body(cq_ref, wt_ref, cos_ref, sin_ref, q_ref, *, heads, scale):
    qt = q_ref.shape[2]
    wins = _row_windows(cq_ref.shape[0], qt)
    accs = [_dot_nt(wt_ref[...], cq_ref[w, :]) for w in wins]
    for w, at in zip(wins, accs):
        cos, sin = cos_ref[:ROT_DIM, w], sin_ref[:ROT_DIM, w]
        for h in range(heads):
            lo = 2 * LANE * h
            pe = _rope_t(at[lo + LANE:lo + LANE + ROT_DIM], cos, sin)
            blk = jnp.concatenate([at[lo:lo + LANE], pe, at[lo + LANE + ROT_DIM:lo + 2 * LANE]], axis=0) * scale
            _store_col_tiles(q_ref.at[:, lo:lo + 2 * LANE, :], w, qt, blk)


def _mla_q_call(cq, w_uq_pad_t, cos_tt, sin_tt, heads, g):
    m, rq = cq.shape
    n = w_uq_pad_t.shape[0]
    tm = _tile(math.gcd(g.mp, g.ts), 512, max(g.qt, LANE))
    const = lambda i: (0, 0)
    scale = (MLA_NOPE + MLA_ROPE) ** -0.5 * LOG2E
    return pl.pallas_call(
        functools.partial(_mla_q_body, heads=heads, scale=scale),
        grid=(m // tm,),
        in_specs=[pl.BlockSpec((tm, rq), lambda i: (i, 0)),
                  pl.BlockSpec((n, rq), const, pipeline_mode=pl.Buffered(1)),
                  pl.BlockSpec((LANE, tm), lambda i: (0, i)), pl.BlockSpec((LANE, tm), lambda i: (0, i))],
        out_specs=pl.BlockSpec((tm // g.qt, n, g.qt), lambda i: (i, 0, 0)),
        out_shape=jax.ShapeDtypeStruct((m // g.qt, n, g.qt), BF16),
        compiler_params=_params(("arbitrary",), rq * n * 2 + 2 * tm * (rq * 2 + n * 2) + 4 * tm * n * 4 + 2**23),
        name="mla_q_up",
    )(cq, w_uq_pad_t, cos_tt, sin_tt)


def _kv_up_body(x_ref, kpe_ref, w_ref, kcat_ref, vt_ref, *, heads):
    wins = _row_windows(x_ref.shape[0], LANE)
    accs = [_dot(x_ref[w, :].astype(BF16), w_ref[...]) for w in wins]
    for w, a in zip(wins, accs):
        kpe = kpe_ref[w, :]
        for h in range(heads):
            lo = 2 * LANE * h
            kcat_ref[w, lo:lo + LANE] = a[:, lo:lo + LANE].astype(BF16)
            kcat_ref[w, lo + LANE:lo + 2 * LANE] = kpe
            vt_ref[h * LANE:(h + 1) * LANE, w] = a[:, lo + LANE:lo + 2 * LANE].T.astype(BF16)


def _kv_up_call(x, kpe16, w_all, j, heads, name):
    m, k = x.shape
    n = w_all.shape[2]
    tm = _tile(m, 512)
    rows = lambda i: (i, 0)
    return pl.pallas_call(
        functools.partial(_kv_up_body, heads=heads),
        grid=(m // tm,),
        in_specs=[pl.BlockSpec((tm, k), rows), pl.BlockSpec((tm, LANE), rows),
                  pl.BlockSpec((None, k, n), lambda i: (j, 0, 0), pipeline_mode=pl.Buffered(1))],
        out_specs=[pl.BlockSpec((tm, 2 * heads * LANE), rows), pl.BlockSpec((heads * LANE, tm), lambda i: (0, i))],
        out_shape=[jax.ShapeDtypeStruct((m, 2 * heads * LANE), BF16), jax.ShapeDtypeStruct((heads * LANE, m), BF16)],
        compiler_params=_params(("arbitrary",), k * n * 2 + 2 * tm * (k * 4 + n * 3) + tm * n * 4 + 2**23),
        name=name,
    )(x, kpe16, w_all)


def _row_fold(x, op):
    out = x[:SUBLANE]
    for r in range(1, x.shape[0] // SUBLANE):
        out = op(out, x[r * SUBLANE:(r + 1) * SUBLANE])
    return out


def _chunks(segments):
    out = []
    for k_ref, vt_ref in segments:
        n = k_ref.shape[0]
        assert n % LANE == 0
        out += [(k_ref, vt_ref, r0, min(KV_CHUNK, n - r0)) for r0 in range(0, n, KV_CHUNK)]
    return out


def _scores_pass(qt, segments, s_ref):
    m = None
    off = 0
    for k_ref, _, r0, n in _chunks(segments):
        s = _dot(k_ref[r0:r0 + n, :], qt)
        s_ref[off:off + n, :] = s
        cm = _row_fold(s, jnp.maximum)
        m = cm if m is None else jnp.maximum(m, cm)
        off += n
    return jnp.max(m, axis=0, keepdims=True)


def _values_pass(segments, s_ref, m):
    acc = l = None
    off = 0
    for _, vt_ref, r0, n in _chunks(segments):
        e = jnp.exp2(s_ref[off:off + n, :] - m)
        cl = _row_fold(e, jnp.add)
        pv = _dot(vt_ref[:, r0:r0 + n], e.astype(BF16))
        l = cl if l is None else l + cl
        acc = pv if acc is None else acc + pv
        off += n
    return acc * (1.0 / jnp.sum(l, axis=0, keepdims=True))


def _pipelined_tiles(n_tiles, scores_of, values_of, s_a, s_b):
    if n_tiles == 1:
        values_of(0, s_a, scores_of(0, s_a))
        return
    unroll = TILE_UNROLL if n_tiles % TILE_UNROLL == 0 else 2
    assert n_tiles % unroll == 0 and unroll % 2 == 0
    bufs = (s_a, s_b)

    def body(j, m):
        t = unroll * j
        for u in range(unroll):
            nxt = t + u + 1 if u + 1 < unroll else jnp.minimum(t + unroll, n_tiles - 1)
            m_nxt = scores_of(nxt, bufs[(u + 1) % 2])
            values_of(t + u, bufs[u % 2], m)
            m = m_nxt
        return m

    lax.fori_loop(0, n_tiles // unroll, body, scores_of(0, s_a))


def _tile_rows(t, n):
    return pl.ds(t * n, n) if isinstance(t, int) else pl.ds(pl.multiple_of(t * n, n), n)


def _pipelined_static(tiles, scores_of, values_of, bufs):
    grp = len(bufs) // 2
    groups = [tiles[i:i + grp] for i in range(0, len(tiles), grp)]
    prev = None
    for gi, group in enumerate(groups):
        half = bufs[(gi % 2) * grp:(gi % 2 + 1) * grp]
        cur = [(tile, buf, scores_of(tile, buf)) for tile, buf in zip(group, half)]
        for tile, buf, m in prev or ():
            values_of(tile, buf, m)
        prev = cur
    for tile, buf, m in prev:
        values_of(tile, buf, m)


def _mla_attn_ctx_body(q_ref, k_ref, vt_ref, o_ref, *bufs, heads):
    nq = q_ref.shape[2]
    seg = lambda h: [(k_ref.at[:, 2 * LANE * h:2 * LANE * (h + 1)], vt_ref.at[LANE * h:LANE * (h + 1), :])]

    def scores_of(tile, s_ref):
        h, r = tile
        return _scores_pass(q_ref[r, 2 * LANE * h:2 * LANE * (h + 1), :], seg(h), s_ref)

    def values_of(tile, s_ref, m):
        h, r = tile
        o_ref[r * nq:(r + 1) * nq, LANE * h:LANE * (h + 1)] = _values_pass(seg(h), s_ref, m).T.astype(BF16)

    tiles = [(h, r) for h in range(heads) for r in range(q_ref.shape[0])]
    _pipelined_static(tiles, scores_of, values_of, bufs)


def _mla_attn_lat_body(q_ref, kc_ref, vc_ref, kn_ref, vn_ref, o_alias, o_ref, s_a, s_b):
    del o_alias
    segs = [(kc_ref, vc_ref), (kn_ref, vn_ref)]
    nq = q_ref.shape[2]

    def scores_of(t, s_ref):
        return _scores_pass(q_ref[t], segs, s_ref)

    def values_of(t, s_ref, m):
        o_ref[_tile_rows(t, nq), :] = _values_pass(segs, s_ref, m).T.astype(BF16)

    _pipelined_tiles(q_ref.shape[0], scores_of, values_of, s_a, s_b)


def _mla_attn(q, kcat, vt, kcat_c, vt_c, heads, g):
    m = kcat.shape[0]
    hn = heads * LANE
    past = kcat_c.shape[0] // g.bs
    o = pl.pallas_call(
        functools.partial(_mla_attn_ctx_body, heads=heads),
        grid=(g.b,),
        in_specs=[pl.BlockSpec((g.s // g.qt, 2 * hn, g.qt), lambda b: (b, 0, 0)),
                  pl.BlockSpec((g.s, 2 * hn), lambda b: (b, 0)),
                  pl.BlockSpec((hn, g.s), lambda b: (0, b))],
        out_specs=pl.BlockSpec((g.s, hn), lambda b: (b, 0)),
        out_shape=jax.ShapeDtypeStruct((m, hn), BF16),
        scratch_shapes=[pltpu.VMEM((g.s, g.qt), F32)] * (2 * CTX_GROUP),
        compiler_params=_params(("arbitrary",), 2 * g.s * hn * 12 + 2**24),
        name="mla_attn_ctx",
    )(q, kcat, vt)
    kb0 = g.mp // g.ts
    return pl.pallas_call(
        _mla_attn_lat_body,
        grid=(g.bs, heads),
        in_specs=[pl.BlockSpec((g.ts // g.qt, 2 * LANE, g.qt), lambda b, h: (kb0 + b, h, 0)),
                  pl.BlockSpec((past, 2 * LANE), lambda b, h: (b, h)),
                  pl.BlockSpec((LANE, past), lambda b, h: (h, b)),
                  pl.BlockSpec((g.ts, 2 * LANE), lambda b, h: (kb0 + b, h)),
                  pl.BlockSpec((LANE, g.ts), lambda b, h: (h, kb0 + b)),
                  pl.BlockSpec(memory_space=pl.ANY)],
        out_specs=pl.BlockSpec((g.ts, LANE), lambda b, h: (kb0 + b, h)),
        out_shape=jax.ShapeDtypeStruct((m, hn), BF16),
        input_output_aliases={5: 0},
        scratch_shapes=[pltpu.VMEM((past + g.ts, g.qt), F32)] * 2,
        compiler_params=_params(("arbitrary", "arbitrary"), 2**25),
        name="mla_attn_lat",
    )(q, kcat_c, vt_c, kcat, vt, o)


def _diff_lambda(lam_ref, lam_init):
    lv = lam_ref[...]
    s1 = jnp.sum(lv[0:1] * lv[1:2], axis=-1, keepdims=True)
    s2 = jnp.sum(lv[2:3] * lv[3:4], axis=-1, keepdims=True)
    return jnp.exp(s1) - jnp.exp(s2) + lam_init


def _diff_queries_t(qt):
    qf = qt.astype(F32)
    row = lax.broadcasted_iota(jnp.int32, qf.shape, 0)
    return jnp.concatenate([jnp.where(row < DIFF_DK, qf, 0.0), jnp.where(row < DIFF_DK, 0.0, qf)],
                           axis=1).astype(BF16)


def _diff_combine(ot, lam, subln_col, lam_init):
    n = ot.shape[1] // 2
    dt = ot[:, :n] - lam * ot[:, n:]
    dt = dt * lax.rsqrt(jnp.mean(dt * dt, axis=0, keepdims=True) + NORM_EPS) * subln_col * (1.0 - lam_init)
    return dt.T.astype(BF16)


def _diff_attn_ctx_body(q_ref, k_ref, vt_ref, lam_ref, sub_ref, o_ref, *bufs, heads, lam_init):
    lam = _diff_lambda(lam_ref, lam_init)
    nq = q_ref.shape[2]
    cols = lambda h: slice(LANE * h, LANE * (h + 1))
    seg = lambda h: [(k_ref.at[:, cols(h)], vt_ref.at[cols(h), :])]

    def scores_of(tile, s_ref):
        h, r = tile
        return _scores_pass(_diff_queries_t(q_ref[r, cols(h), :]), seg(h), s_ref)

    def values_of(tile, s_ref, m):
        h, r = tile
        o_ref[r * nq:(r + 1) * nq, cols(h)] = _diff_combine(_values_pass(seg(h), s_ref, m), lam, sub_ref[...],
                                                           lam_init)

    tiles = [(h, r) for h in range(heads) for r in range(q_ref.shape[0])]
    _pipelined_static(tiles, scores_of, values_of, bufs)


def _diff_attn_lat_body(q_ref, kc_ref, vc_ref, kn_ref, vn_ref, lam_ref, sub_ref, o_alias, o_ref, s_a, s_b, *,
                        lam_init):
    del o_alias
    lam = _diff_lambda(lam_ref, lam_init)
    segs = [(kc_ref, vc_ref), (kn_ref, vn_ref)]
    nq = q_ref.shape[2]

    def scores_of(t, s_ref):
        return _scores_pass(_diff_queries_t(q_ref[t]), segs, s_ref)

    def values_of(t, s_ref, m):
        o_ref[_tile_rows(t, nq), :] = _diff_combine(_values_pass(segs, s_ref, m), lam, sub_ref[...], lam_init)

    _pipelined_tiles(q_ref.shape[0], scores_of, values_of, s_a, s_b)


def _diff_attn(q, k, vt, k_c, vt_c, lam_vecs, subln, lam_init, heads, g):
    m = k.shape[0]
    hn = heads * LANE
    past = k_c.shape[0] // g.bs
    sub = subln.reshape(-1, 1)
    blk = lambda b: (b, 0)
    qd = g.qt // 2
    o = pl.pallas_call(
        functools.partial(_diff_attn_ctx_body, heads=heads, lam_init=lam_init),
        grid=(g.b,),
        in_specs=[pl.BlockSpec((g.s // qd, hn, qd), lambda b: (b, 0, 0)), pl.BlockSpec((g.s, hn), blk),
                  pl.BlockSpec((hn, g.s), lambda b: (0, b)),
                  pl.BlockSpec((4, DIFF_DK), lambda b: (0, 0)), pl.BlockSpec((DIFF_DV, 1), lambda b: (0, 0))],
        out_specs=pl.BlockSpec((g.s, hn), blk),
        out_shape=jax.ShapeDtypeStruct((m, hn), BF16),
        scratch_shapes=[pltpu.VMEM((g.s, g.qt), F32)] * (2 * CTX_GROUP),
        compiler_params=_params(("arbitrary",), 2 * g.s * hn * 12 + 2**24),
        name="diff_attn_ctx",
    )(q, k, vt, lam_vecs, sub)
    kb0 = g.mp // g.ts
    return pl.pallas_call(
        functools.partial(_diff_attn_lat_body, lam_init=lam_init),
        grid=(g.bs, heads),
        in_specs=[pl.BlockSpec((g.ts // qd, LANE, qd), lambda b, h: (kb0 + b, h, 0)),
                  pl.BlockSpec((past, LANE), lambda b, h: (b, h)),
                  pl.BlockSpec((LANE, past), lambda b, h: (h, b)),
                  pl.BlockSpec((g.ts, LANE), lambda b, h: (kb0 + b, h)),
                  pl.BlockSpec((LANE, g.ts), lambda b, h: (h, kb0 + b)),
                  pl.BlockSpec((4, DIFF_DK), lambda b, h: (0, 0)),
                  pl.BlockSpec((DIFF_DV, 1), lambda b, h: (0, 0)),
                  pl.BlockSpec(memory_space=pl.ANY)],
        out_specs=pl.BlockSpec((g.ts, LANE), lambda b, h: (kb0 + b, h)),
        out_shape=jax.ShapeDtypeStruct((m, hn), BF16),
        input_output_aliases={7: 0},
        scratch_shapes=[pltpu.VMEM((past + g.ts, g.qt), F32)] * 2,
        compiler_params=_params(("arbitrary", "arbitrary"), 2**25),
        name="diff_attn_lat",
    )(q, k_c, vt_c, k, vt, lam_vecs, sub, o)


def _proj_body(h_ref, w_ref, cos_ref, sin_ref, *refs, rope, scale, kinds, seq, n_ctx_tiles, n_alias):
    out_refs = refs[n_alias:]
    i = pl.program_id(1)
    wins = _row_windows(h_ref.shape[0], max(seq, LANE))
    accs = [_dot(h_ref[w, :], w_ref[...]) for w in wins]
    for w, a in zip(wins, accs):
        if rope:
            reps = a.shape[1] // LANE
            a = _rope(a, _lane_tile(cos_ref[w, :], reps), _lane_tile(sin_ref[w, :], reps))
        if scale != 1.0:
            a = a * scale
        at = a.T if any(k != "rows" for k in kinds) else None
        for o_ref, kind in zip(out_refs, kinds):
            if kind == "rows":
                o_ref[w, :] = a.astype(o_ref.dtype)
            elif kind == "cols":
                o_ref[:, w] = at.astype(o_ref.dtype)
            elif kind == "col_tiles":
                _store_col_tiles(o_ref, w, o_ref.shape[2], at)
            else:
                @pl.when(i < n_ctx_tiles)
                def _(o_ref=o_ref, kind=kind, a=a, at=at, w=w):
                    _store_state(o_ref, w, seq, at if kind == "state_cols" else a, kind == "state_cols")


def _proj_call(h, w_all, layer, col0, ncols, cos_t, sin_t, rope, scale, outs, st_layer, st_layers, state, g, name):
    m, d = h.shape
    tn = _tile(ncols, 1024, LANE)
    tm = _tile(math.gcd(g.mp, g.ts), 512, LANE)
    assert tm % g.s == 0
    nct = g.mp // tm
    nb0 = col0 // tn
    colb = lambda j, i: (i, j)
    specs, shapes = [], []
    for kind, dt in outs:
        if kind == "rows":
            specs.append(pl.BlockSpec((tm, tn), lambda j, i: (i, j)))
            shapes.append(jax.ShapeDtypeStruct((m, ncols), dt))
        elif kind == "cols":
            specs.append(pl.BlockSpec((tn, tm), lambda j, i: (j, i)))
            shapes.append(jax.ShapeDtypeStruct((ncols, m), dt))
        elif kind == "col_tiles":
            qd = g.qt // 2
            specs.append(pl.BlockSpec((tm // qd, tn, qd), lambda j, i: (i, j, 0)))
            shapes.append(jax.ShapeDtypeStruct((m // qd, ncols, qd), dt))
        else:
            tr = kind == "state_cols"
            specs.append(_state_spec(tm, g.s, nct, st_layer, g.s, tn, tr, colb))
            shapes.append(jax.ShapeDtypeStruct((g.b, st_layers, ncols, g.s) if tr else (g.b, st_layers, g.s, ncols), dt))
    st_idx = [k for k, (kind, _) in enumerate(outs) if kind.startswith("state")]
    n_alias = len(st_idx) if state is not None else 0
    return pl.pallas_call(
        functools.partial(_proj_body, rope=rope, scale=scale, kinds=tuple(k for k, _ in outs), seq=g.s,
                          n_ctx_tiles=nct, n_alias=n_alias),
        grid=(ncols // tn, m // tm),
        in_specs=[pl.BlockSpec((tm, d), lambda j, i: (i, 0)),
                  pl.BlockSpec((None, d, tn), lambda j, i: (layer, 0, nb0 + j)),
                  pl.BlockSpec((tm, LANE), lambda j, i: (i, 0)), pl.BlockSpec((tm, LANE), lambda j, i: (i, 0))]
                 + [pl.BlockSpec(memory_space=pl.ANY)] * n_alias,
        out_specs=specs,
        out_shape=shapes,
        input_output_aliases={4 + a: st_idx[a] for a in range(n_alias)},
        compiler_params=_params(("arbitrary", "arbitrary"),
                                d * tn * 4 + 2 * tm * (d * 2 + tn * 6) + 4 * tm * tn * 4 + 2**23),
        name=name,
    )(h, w_all, cos_t, sin_t, *([state] if n_alias else []))


def _out_ln_body(o_ref, w_ref, x_ref, gate_ref, g_ref, b_ref, sc_ref, sh_ref, xo_ref, ho_ref, *, alpha):
    wins = _row_windows(o_ref.shape[0], 16)
    accs = [_dot(o_ref[w, :], w_ref[...]) for w in wins]
    for w, y in zip(wins, accs):
        xn = _layer_norm(alpha * x_ref[w, :] + gate_ref[...] * y, g_ref[...], b_ref[...])
        xo_ref[w, :] = xn
        ho_ref[w, :] = (xn * (1.0 + sc_ref[...]) + sh_ref[...]).astype(BF16)


def _out_ln_call(o, w_o_all, layer, x, mod, ln_g, ln_b, alpha, g):
    m, k = o.shape
    d = w_o_all.shape[2]
    tm = _tile(math.gcd(g.mp, g.ts), 512)
    row = lambda i: _mod_row(i, tm, g.mp, g.ts)
    const = lambda i: (0, 0)
    rows = lambda i: (i, 0)
    return pl.pallas_call(
        functools.partial(_out_ln_body, alpha=alpha),
        grid=(m // tm,),
        in_specs=[pl.BlockSpec((tm, k), rows),
                  pl.BlockSpec((None, k, d), lambda i: (layer, 0, 0), pipeline_mode=pl.Buffered(1)),
                  pl.BlockSpec((tm, d), rows), _mod_spec(2, d, row),
                  pl.BlockSpec((1, d), const), pl.BlockSpec((1, d), const),
                  _mod_spec(4, d, row), _mod_spec(3, d, row)],
        out_specs=[pl.BlockSpec((tm, d), rows), pl.BlockSpec((tm, d), rows)],
        out_shape=[jax.ShapeDtypeStruct((m, d), F32), jax.ShapeDtypeStruct((m, d), BF16)],
        compiler_params=_params(("arbitrary",), k * d * 2 + 2 * tm * (k * 2 + d * 10) + 4 * tm * d * 4 + 2**23),
        name="out_proj_ln",
    )(o, w_o_all, x, mod, _row_vec(ln_g), _row_vec(ln_b), mod, mod)


def _lane_cumsum(x, tri):
    c = tri.shape[0]
    carry = jnp.zeros((x.shape[0], 1), F32)
    parts = []
    for j in range(x.shape[1] // c):
        xc = x[:, j * c:(j + 1) * c]
        parts.append(_dot(xc.astype(BF16), tri) + carry)
        carry = carry + jnp.sum(xc, axis=1, keepdims=True)
    return parts[0] if len(parts) == 1 else jnp.concatenate(parts, axis=1)


def _route_body(h_ref, wr_ref, slot_ref, gate_ref, *, groups, t, cap):
    logits = _dot_nt(wr_ref[...].astype(BF16), h_ref[...])
    ex = jnp.exp(logits - jnp.max(logits, axis=0, keepdims=True))
    aff = ex / jnp.sum(ex, axis=0, keepdims=True)
    nbits = jnp.finfo(aff.dtype).bits
    keys = lax.bitcast_convert_type(aff, jnp.dtype(f"int{nbits}"))
    c = min(t, 2 * LANE)
    tri = jnp.where(lax.broadcasted_iota(jnp.int32, (c, c), 0) <= lax.broadcasted_iota(jnp.int32, (c, c), 1),
                    1.0, 0.0).astype(BF16)
    capf = float(cap)
    for gi in range(groups):
        k = keys[:, gi * t:(gi + 1) * t]
        thr = jnp.zeros((k.shape[0], 1), keys.dtype)
        for bit in range(nbits - 2, -1, -1):
            cand = thr | (1 << bit)
            cnt = jnp.sum(jnp.where(k >= cand, 1.0, 0.0), axis=1, keepdims=True)
            thr = jnp.where(cnt >= capf, cand, thr)
        gt = jnp.where(k > thr, 1.0, 0.0)
        eq = jnp.where(k == thr, 1.0, 0.0)
        need = capf - jnp.sum(gt, axis=1, keepdims=True)
        eq_before = _lane_cumsum(eq, tri) - eq
        sel = gt + eq * jnp.where(eq_before < need, 1.0, 0.0)
        pos = _lane_cumsum(sel, tri) - 1.0
        slot_ref[gi] = jnp.where(sel > 0.5, pos, -1.0).astype(jnp.int32)
        gate_ref[gi] = aff[:, gi * t:(gi + 1) * t]


def _route_call(h2, w_router_t, row0, nreq, t, cap, groups, name):
    e, d = w_router_t.shape
    rb0 = row0 // (groups * t)
    return pl.pallas_call(
        functools.partial(_route_body, groups=groups, t=t, cap=cap),
        grid=(nreq // groups,),
        in_specs=[pl.BlockSpec((groups * t, d), lambda i: (rb0 + i, 0)), pl.BlockSpec((e, d), lambda i: (0, 0))],
        out_specs=[pl.BlockSpec((groups, e, t), lambda i: (i, 0, 0)), pl.BlockSpec((groups, e, t), lambda i: (i, 0, 0))],
        out_shape=[jax.ShapeDtypeStruct((nreq, e, t), jnp.int32), jax.ShapeDtypeStruct((nreq, e, t), F32)],
        compiler_params=_params(("parallel",), 2 * groups * t * d * 2 + 2**24),
        name=name,
    )(h2, w_router_t)


def _one_hot_rows(slot_row, cap):
    r = lax.broadcasted_iota(jnp.int32, (cap, slot_row.shape[1]), 0)
    return slot_row == r


def _gather_ctx_body(h_ref, slot_ref, gate_ref, xs_ref, gs_ref, *, cap):
    ne = slot_ref.shape[0]
    masks = [_one_hot_rows(slot_ref[e:e + 1, :], cap) for e in range(ne)]
    p = jnp.concatenate([jnp.where(mk, 1.0, 0.0) for mk in masks], axis=0).astype(BF16)
    xs = _dot(p, h_ref[...]).astype(BF16)
    for e in range(ne):
        xs_ref[e] = xs[e * cap:(e + 1) * cap]
        gs_ref[e] = jnp.sum(jnp.where(masks[e], gate_ref[e:e + 1, :], 0.0), axis=1, keepdims=True)


def _gather_lat_body(h_ref, slot_ref, gate_ref, xs_alias, gs_alias, xs_ref, gs_ref, *, cap):
    del xs_alias, gs_alias
    e = pl.program_id(1)
    mk = _one_hot_rows(slot_ref[pl.ds(e, 1), :], cap)
    xs_ref[...] = _dot(jnp.where(mk, 1.0, 0.0).astype(BF16), h_ref[...]).astype(BF16)
    gs_ref[...] = jnp.sum(jnp.where(mk, gate_ref[pl.ds(e, 1), :], 0.0), axis=1, keepdims=True)


def _gather(h2, slot_p, gate_p, slot_s, gate_s, g):
    d = h2.shape[1]
    ne = slot_p.shape[1]
    rows_e = g.b * g.cap_p + g.bs * g.cap_s
    shapes = [jax.ShapeDtypeStruct((ne, rows_e, d), BF16), jax.ShapeDtypeStruct((ne, rows_e, 1), F32)]
    xs, gs = pl.pallas_call(
        functools.partial(_gather_ctx_body, cap=g.cap_p),
        grid=(g.b,),
        in_specs=[pl.BlockSpec((g.s, d), lambda b: (b, 0)),
                  pl.BlockSpec((None, ne, g.s), lambda b: (b, 0, 0)), pl.BlockSpec((None, ne, g.s), lambda b: (b, 0, 0))],
        out_specs=[pl.BlockSpec((ne, g.cap_p, d), lambda b: (0, b, 0)), pl.BlockSpec((ne, g.cap_p, 1), lambda b: (0, b, 0))],
        out_shape=shapes,
        compiler_params=_params(("parallel",), 2 * ne * g.cap_p * (d * 8 + g.s * 8) + 2**24),
        name="moe_gather_ctx",
    )(h2, slot_p, gate_p)
    blk0 = (g.b * g.cap_p) // g.cap_s
    hb0 = g.mp // g.ts
    return pl.pallas_call(
        functools.partial(_gather_lat_body, cap=g.cap_s),
        grid=(g.bs, ne),
        in_specs=[pl.BlockSpec((g.ts, d), lambda b, e: (hb0 + b, 0)),
                  pl.BlockSpec((None, ne, g.ts), lambda b, e: (b, 0, 0)), pl.BlockSpec((None, ne, g.ts), lambda b, e: (b, 0, 0)),
                  pl.BlockSpec(memory_space=pl.ANY), pl.BlockSpec(memory_space=pl.ANY)],
        out_specs=[pl.BlockSpec((None, g.cap_s, d), lambda b, e: (e, blk0 + b, 0)),
                   pl.BlockSpec((None, g.cap_s, 1), lambda b, e: (e, blk0 + b, 0))],
        out_shape=shapes,
        input_output_aliases={3: 0, 4: 1},
        compiler_params=_params(("parallel", "parallel"), 2 * g.ts * d * 2 + g.cap_s * (g.ts * 12 + d * 12) + 2**24),
        name="moe_gather_lat",
    )(h2, slot_s, gate_s, xs, gs)


def _ffn1_body(x_ref, wg_ref, wu_ref, o_ref):
    x = x_ref[...]
    a = _dot(x, wg_ref[...].astype(BF16))
    u = _dot(x, wu_ref[...].astype(BF16))
    o_ref[...] = (a * _sigmoid(a) * u).astype(BF16)


def _ffn1_call(xs, w_gate, w_up, layer):
    ne, rows, d = xs.shape
    f = w_gate.shape[3]
    tn = _tile(f, 256, LANE)
    wspec = pl.BlockSpec((None, None, d, tn), lambda e, n: (layer, e, 0, n))
    return pl.pallas_call(
        _ffn1_body,
        grid=(ne, f // tn),
        in_specs=[pl.BlockSpec((None, rows, d), lambda e, n: (e, 0, 0)), wspec, wspec],
        out_specs=pl.BlockSpec((None, rows, tn), lambda e, n: (e, 0, n)),
        out_shape=jax.ShapeDtypeStruct((ne, rows, f), BF16),
        compiler_params=_params(("parallel", "parallel"), 2 * rows * d * 2 + 4 * d * tn * 4 + 2 * d * tn * 2 + 5 * rows * tn * 4 + 2**23),
        name="moe_ffn_gate_up",
    )(xs, w_gate, w_up)


def _ffn2_body(h_ref, w_ref, gs_ref, o_ref):
    o_ref[...] = (_dot(h_ref[...], w_ref[...].astype(BF16)) * gs_ref[...]).astype(BF16)


def _ffn2_call(hm, w_down, gs, layer):
    ne, rows, f = hm.shape
    d = w_down.shape[3]
    tn = _tile(d, 256, LANE)
    return pl.pallas_call(
        _ffn2_body,
        grid=(ne, d // tn),
        in_specs=[pl.BlockSpec((None, rows, f), lambda e, n: (e, 0, 0)),
                  pl.BlockSpec((None, None, f, tn), lambda e, n: (layer, e, 0, n)),
                  pl.BlockSpec((None, rows, 1), lambda e, n: (e, 0, 0))],
        out_specs=pl.BlockSpec((None, rows, tn), lambda e, n: (e, 0, n)),
        out_shape=jax.ShapeDtypeStruct((ne, rows, d), BF16),
        compiler_params=_params(("parallel", "parallel"), 2 * rows * f * 2 + 2 * f * tn * 4 + f * tn * 2 + 3 * rows * tn * 4 + 2 * rows * LANE * 4 + 2**23),
        name="moe_ffn_down",
    )(hm, w_down, gs)


def _combine(y_ref, slot, cap):
    ne = y_ref.shape[0]
    p = jnp.concatenate([jnp.where(_one_hot_rows(slot[e:e + 1, :], cap), 1.0, 0.0) for e in range(ne)], axis=0)
    y = y_ref[...].reshape(ne * cap, y_ref.shape[2])
    return _dot_tn(p.astype(BF16), y)


def _combine_ln_body(*refs, cap, alpha, n_alias, emit_h):
    y_ref, slot_ref, x_ref, gate_ref, g_ref, b_ref = refs[:6]
    rest = refs[6:]
    if emit_h:
        sc_ref, sh_ref = rest[:2]
        rest = rest[2:]
    outs = rest[n_alias:]
    moe = _combine(y_ref, slot_ref[...], cap)
    xn = _layer_norm(alpha * x_ref[...] + gate_ref[...] * moe, g_ref[...], b_ref[...])
    outs[0][...] = xn
    if emit_h:
        outs[1][...] = (xn * (1.0 + sc_ref[...]) + sh_ref[...]).astype(BF16)


def _combine_ln(y, slot_p, slot_s, x, mod, mod_next, ln_g, ln_b, alpha, g):
    ne, rows_e, d = y.shape
    m = x.shape[0]
    emit_h = mod_next is not None
    lg, lb = _row_vec(ln_g), _row_vec(ln_b)
    shapes = [jax.ShapeDtypeStruct((m, d), F32), jax.ShapeDtypeStruct((m, d), BF16)] if emit_h else \
        [jax.ShapeDtypeStruct((g.mp, d), F32)]
    n_out = len(shapes)

    def specs(row):
        sp = [pl.BlockSpec((1, d), lambda *a: (0, 0)), pl.BlockSpec((1, d), lambda *a: (0, 0))]
        md = [_mod_spec(5, d, row)]
        nx = [_mod_spec(1, d, row), _mod_spec(0, d, row)] if emit_h else []
        return md, sp, nx

    md, sp, nx = specs(lambda b: 0)
    outs = pl.pallas_call(
        functools.partial(_combine_ln_body, cap=g.cap_p, alpha=alpha, n_alias=0, emit_h=emit_h),
        grid=(g.b,),
        in_specs=[pl.BlockSpec((ne, g.cap_p, d), lambda b: (0, b, 0)),
                  pl.BlockSpec((None, ne, g.s), lambda b: (b, 0, 0)),
                  pl.BlockSpec((g.s, d), lambda b: (b, 0))] + md + sp + nx,
        out_specs=[pl.BlockSpec((g.s, d), lambda b: (b, 0))] * n_out,
        out_shape=shapes,
        compiler_params=_params(("parallel",), 2 * ne * g.cap_p * (d * 2 + g.s * 8) + g.s * d * 40 + 2**24),
        name="moe_combine_ln_ctx",
    )(y, slot_p, x, mod, lg, lb, *([mod_next, mod_next] if emit_h else []))
    outs = list(outs) if isinstance(outs, (list, tuple)) else [outs]
    tt = _tile(g.ts, 256, LANE)
    nt = g.ts // tt
    blk0 = (g.b * g.cap_p) // g.cap_s
    xb0 = g.mp // tt
    md, sp, nx = specs(lambda b, i: 1 + b)
    n_in = 6 + len(nx)
    n_alias = n_out if emit_h else 0
    ob0 = xb0 if emit_h else 0
    outs_s = pl.pallas_call(
        functools.partial(_combine_ln_body, cap=g.cap_s, alpha=alpha, n_alias=n_alias, emit_h=emit_h),
        grid=(g.bs, nt),
        in_specs=[pl.BlockSpec((ne, g.cap_s, d), lambda b, i: (0, blk0 + b, 0), pipeline_mode=pl.Buffered(1)),
                  pl.BlockSpec((None, ne, tt), lambda b, i: (b, 0, i)),
                  pl.BlockSpec((tt, d), lambda b, i: (xb0 + b * nt + i, 0))] + md + sp + nx
                 + [pl.BlockSpec(memory_space=pl.ANY)] * n_alias,
        out_specs=[pl.BlockSpec((tt, d), lambda b, i: (ob0 + b * nt + i, 0))] * n_out,
        out_shape=shapes if emit_h else [jax.ShapeDtypeStruct((g.ms, d), F32)],
        input_output_aliases={n_in + j: j for j in range(n_alias)},
        compiler_params=_params(("parallel", "parallel"), ne * g.cap_s * (d * 2 + tt * 10) + tt * d * 48 + 2**24),
        name="moe_combine_ln_lat",
    )(y, slot_s, x, mod, lg, lb, *([mod_next, mod_next] if emit_h else []), *(outs if emit_h else []))
    outs_s = list(outs_s) if isinstance(outs_s, (list, tuple)) else [outs_s]
    if emit_h:
        return outs_s[0], outs_s[1]
    return outs[0], outs_s[0]


class _Geom:
    def __init__(self, b, s, bs, ts, ne):
        self.b, self.s, self.bs, self.ts = b, s, bs, ts
        self.mp, self.ms = b * s, bs * ts
        self.cap_p = CAP_FACTOR * s // ne
        self.cap_s = CAP_FACTOR * ts // ne
        self.qt = min(Q_TILE, s)
        assert ts % s == 0 and self.mp % ts == 0 and bs + 1 <= MOD_ROWS
        assert (b * self.cap_p) % self.cap_s == 0 and self.cap_p % 16 == 0


def _rope_tables(g):
    rows = g.ts // GRID_W
    row = jnp.repeat(jnp.arange(rows, dtype=F32), GRID_W)
    col = jnp.tile(jnp.arange(GRID_W, dtype=F32), rows)
    half = ROT_DIM // 2
    inv_freq = ROPE_BASE ** (-jnp.arange(0, half, 2, dtype=F32) / half)
    ang_r = row[:, None] * inv_freq
    ang_c = col[:, None] * inv_freq
    ang = jnp.concatenate([ang_r, ang_r, ang_c, ang_c] * (LANE // ROT_DIM), axis=-1)
    cos = jnp.concatenate([jnp.ones((g.mp, LANE), F32), jnp.tile(jnp.cos(ang), (g.bs, 1))], axis=0)
    sin = jnp.concatenate([jnp.zeros((g.mp, LANE), F32), jnp.tile(jnp.sin(ang), (g.bs, 1))], axis=0)
    return cos, sin


def _pad_heads(w, heads, width):
    k = w.shape[0]
    per = w.shape[1] // heads
    return jnp.pad(w.reshape(k, heads, per), ((0, 0), (0, 0), (0, width - per))).reshape(k, heads * width)


def kernel(x_prompt, x_sample, cache_mla_ckv, cache_mla_kpe, cache_diff_k, cache_diff_v, c, c_ctx, ada_w, ada_b, ln1_g, ln1_b, ln2_g, ln2_b, mla_w_dq, mla_q_norm, mla_w_uq, mla_w_dkv, mla_kv_norm, mla_w_ukv, mla_w_o, diff_w_qkv, diff_lambda_q1, diff_lambda_k1, diff_lambda_q2, diff_lambda_k2, diff_subln, diff_w_o, moe_w_router, moe_w_gate, moe_w_up, moe_w_down):
    b, s, d = x_prompt.shape
    bs, ts, _ = x_sample.shape
    depth = ada_w.shape[0]
    ne = moe_w_router.shape[-1]
    g = _Geom(b, s, bs, ts, ne)
    mla_heads = mla_w_o.shape[1] // MLA_V
    diff_heads = diff_w_o.shape[1] // DIFF_DV
    alpha = (2.0 * depth) ** 0.25
    past = cache_mla_ckv.shape[2]

    x = jnp.concatenate([x_prompt.reshape(g.mp, d), x_sample.reshape(g.ms, d)], axis=0)
    cv = jnp.concatenate([c_ctx[None, :], c, jnp.zeros((MOD_ROWS - 1 - bs, d), F32)], axis=0)
    mods = _mod_call(cv, ada_w, ada_b)
    mod_of = lambda l: mods[l].reshape(MOD_ROWS, 1, 6 * d)
    cos_t, sin_t = _rope_tables(g)
    h = _modcast_call(x, mod_of(0), g)

    n_mla, n_diff = mla_w_o.shape[0], diff_w_o.shape[0]
    mla_w_ukv, mla_w_o, diff_w_qkv, diff_w_o = (w.astype(BF16) for w in (mla_w_ukv, mla_w_o, diff_w_qkv, diff_w_o))
    st_mla = st_dk = st_dv = None
    for l in range(depth):
        mod = mod_of(l)
        j = l // 2
        if l % 2 == 0:
            w_down = jnp.pad(jnp.concatenate([mla_w_dq[j], mla_w_dkv[j]], axis=1),
                             ((0, 0), (0, LANE - MLA_ROPE))).astype(BF16)
            cq, ckv, kpe16, *st_mla = _mla_down_call(h, w_down, mla_q_norm[j], mla_kv_norm[j], cos_t, sin_t,
                                                     j, n_mla, st_mla, g)
            w_uq_t = _pad_heads(mla_w_uq[j], mla_heads, 2 * LANE).T.astype(BF16)
            q = _mla_q_call(cq, w_uq_t, cos_t.T, sin_t.T, mla_heads, g)
            kpe_c = jnp.pad(cache_mla_kpe[:, j], ((0, 0), (0, 0), (0, LANE - MLA_ROPE))).astype(BF16)
            kcat, vt = _kv_up_call(ckv, kpe16, mla_w_ukv, j, mla_heads, "mla_kv_up")
            kcat_c, vt_c = _kv_up_call(cache_mla_ckv[:, j].reshape(bs * past, -1), kpe_c.reshape(bs * past, LANE),
                                       mla_w_ukv, j, mla_heads, "mla_kv_up_cache")
            o = _mla_attn(q, kcat, vt, kcat_c, vt_c, mla_heads, g)
            w_o = mla_w_o
        else:
            lam_init = 0.8 - 0.6 * math.exp(-0.3 * l)
            nq = diff_heads * 2 * DIFF_DK
            nv = diff_heads * DIFF_DV
            q, = _proj_call(h, diff_w_qkv, j, 0, nq, cos_t, sin_t, True, DIFF_DK ** -0.5 * LOG2E,
                            [("col_tiles", BF16)], j, n_diff, None, g, "diff_q")
            k16, st_dk = _proj_call(h, diff_w_qkv, j, nq, nq, cos_t, sin_t, True, 1.0,
                                    [("rows", BF16), ("state_cols", F32)], j, n_diff, st_dk, g, "diff_k")
            vt16, st_dv = _proj_call(h, diff_w_qkv, j, 2 * nq, nv, cos_t, sin_t, False, 1.0,
                                     [("cols", BF16), ("state_rows", F32)], j, n_diff, st_dv, g, "diff_v")
            k_c = cache_diff_k[:, j].reshape(bs * past, nq).astype(BF16)
            vt_c = cache_diff_v[:, j].reshape(bs * past, nv).T.astype(BF16)
            lam_vecs = jnp.stack([diff_lambda_q1[j], diff_lambda_k1[j], diff_lambda_q2[j], diff_lambda_k2[j]])
            o = _diff_attn(q, k16, vt16, k_c, vt_c, lam_vecs, diff_subln[j], lam_init, diff_heads, g)
            w_o = diff_w_o
        x, h2 = _out_ln_call(o, w_o, j, x, mod, ln1_g[l], ln1_b[l], alpha, g)

        w_rt = moe_w_router[l].T
        groups = max(1, min(g.b, g.ts // g.s))
        slot_p, gate_p = _route_call(h2, w_rt, 0, g.b, g.s, g.cap_p, groups, "moe_route_ctx")
        slot_s, gate_s = _route_call(h2, w_rt, g.mp, g.bs, g.ts, g.cap_s, 1, "moe_route_lat")
        xs, gs = _gather(h2, slot_p, gate_p, slot_s, gate_s, g)
        hm = _ffn1_call(xs, moe_w_gate, moe_w_up, l)
        y = _ffn2_call(hm, moe_w_down, gs, l)
        mod_next = mod_of(l + 1) if l + 1 < depth else None
        x, h = _combine_ln(y, slot_p, slot_s, x, mod, mod_next, ln2_g[l], ln2_b[l], alpha, g)

    st_ckv, st_kpe = st_mla
    return (x.reshape(b, s, d), h.reshape(bs, ts, d), st_ckv, jnp.swapaxes(st_kpe, 2, 3),
            jnp.moveaxis(st_dk.reshape(b, n_diff, diff_heads, 2, DIFF_DK, s), 5, 2),
            st_dv.reshape(b, n_diff, s, diff_heads, DIFF_DV))
```

```python
import functools
import math

import jax
import jax.numpy as jnp
from jax import lax
from jax.experimental import pallas as pl
from jax.experimental.pallas import tpu as pltpu

F32 = jnp.float32
BF16 = jnp.bfloat16

NORM_EPS = 1e-6
ROPE_BASE = 10000.0
GRID_W = 64
CAP_FACTOR = 2
MLA_NOPE, MLA_ROPE, MLA_V = 128, 64, 128
DIFF_DK, DIFF_DV = 64, 128
ROT_DIM = 64
LANE = 128
MOD_ROWS = 8
LOG2E = math.log2(math.e)
SUBLANE = 8
Q_TILE = 256
KV_CHUNK = 256
ROW_SPLIT = 2
TILE_UNROLL = 4
CTX_GROUP = 4
V7X_VMEM_BYTES = 64 * 2**20
VMEM_HEADROOM_BYTES = 6 * 2**20


def _params(semantics, vmem_bytes):
    limit = min(int(vmem_bytes), V7X_VMEM_BYTES - VMEM_HEADROOM_BYTES)
    return pltpu.CompilerParams(dimension_semantics=semantics, vmem_limit_bytes=limit)


def _tile(n, target, mult=16):
    t = min(n, target)
    while t > mult and (n % t or t % mult):
        t -= mult
    assert n % t == 0 and t % mult == 0, (n, target, mult)
    return t


def _dot(a, b):
    return jnp.dot(a, b, preferred_element_type=F32)


def _dot_nt(a, b):
    return lax.dot_general(a, b, (((1,), (1,)), ((), ())), preferred_element_type=F32)


def _dot_tn(a, b):
    return lax.dot_general(a, b, (((0,), (0,)), ((), ())), preferred_element_type=F32)


def _sigmoid(x):
    return 1.0 / (1.0 + jnp.exp(-x))


def _rms(x):
    return x * lax.rsqrt(jnp.mean(x * x, axis=-1, keepdims=True) + NORM_EPS)


def _layer_norm(z, g, b):
    mu = jnp.mean(z, axis=-1, keepdims=True)
    zc = z - mu
    var = jnp.mean(zc * zc, axis=-1, keepdims=True)
    return zc * lax.rsqrt(var + NORM_EPS) * g + b


def _rope(x, cos, sin):
    n = x.shape[-1]
    nxt = pltpu.roll(x, n - 16, 1)
    prv = pltpu.roll(x, 16, 1)
    lane = lax.broadcasted_iota(jnp.int32, x.shape, 1)
    rot = jnp.where((lane % 32) < 16, -nxt, prv)
    return x * cos + rot * sin


def _mod_row(i, tm, mp, ts):
    r0 = i * tm
    return jnp.where(r0 < mp, 0, 1 + (r0 - mp) // ts)


def _mod_spec(which, d, row_fn):
    return pl.BlockSpec((None, 1, d), lambda *ids: (row_fn(*ids), 0, which))


def _row_vec(v):
    return v.reshape(1, -1)


def _mod_body(c_ref, w_ref, b_ref, o_ref):
    c = c_ref[...]
    s = (c * _sigmoid(c)).astype(BF16)
    o_ref[...] = _dot(s, w_ref[...].astype(BF16)) + b_ref[...]


def _mod_call(cv, ada_w, ada_b):
    nl, d, n6 = ada_w.shape
    tn = _tile(n6, 1024, LANE)
    return pl.pallas_call(
        _mod_body,
        grid=(nl, n6 // tn),
        in_specs=[pl.BlockSpec((MOD_ROWS, d), lambda l, n: (0, 0)),
                  pl.BlockSpec((None, d, tn), lambda l, n: (l, 0, n)),
                  pl.BlockSpec((None, 1, tn), lambda l, n: (l, 0, n))],
        out_specs=pl.BlockSpec((None, MOD_ROWS, tn), lambda l, n: (l, 0, n)),
        out_shape=jax.ShapeDtypeStruct((nl, MOD_ROWS, n6), F32),
        compiler_params=_params(("parallel", "parallel"), 4 * d * tn * 4 + 2**22),
        name="ada_mod",
    )(cv, ada_w, ada_b.reshape(nl, 1, n6))


def _modcast_body(x_ref, sc_ref, sh_ref, o_ref):
    o_ref[...] = (x_ref[...] * (1.0 + sc_ref[...]) + sh_ref[...]).astype(BF16)


def _modcast_call(x, mod, g):
    m, d = x.shape
    tm = _tile(math.gcd(g.mp, g.ts), 512)
    row = lambda i: _mod_row(i, tm, g.mp, g.ts)
    return pl.pallas_call(
        _modcast_body,
        grid=(m // tm,),
        in_specs=[pl.BlockSpec((tm, d), lambda i: (i, 0)), _mod_spec(1, d, row), _mod_spec(0, d, row)],
        out_specs=pl.BlockSpec((tm, d), lambda i: (i, 0)),
        out_shape=jax.ShapeDtypeStruct((m, d), BF16),
        compiler_params=_params(("parallel",), 2 * tm * d * 6 + 2**22),
        name="modulate_in",
    )(x, mod, mod)


def _row_windows(tm, unit):
    n = ROW_SPLIT if tm % (ROW_SPLIT * unit) == 0 else 1
    return [slice(r, r + tm // n) for r in range(0, tm, tm // n)]


def _store_state(st_ref, win, seq, value, transposed):
    for r in range((win.stop - win.start) // seq):
        tok = slice(r * seq, (r + 1) * seq)
        st_ref[win.start // seq + r] = value[:, tok] if transposed else value[tok]


def _state_spec(tm, seq, n_ctx_tiles, layer, rows, cols, transposed, col_block):
    shape = (tm // seq, None, cols, rows) if transposed else (tm // seq, None, rows, cols)

    def index(*ids):
        i, jn = col_block(*ids)
        tile = jnp.minimum(i, n_ctx_tiles - 1)
        return (tile, layer, jn, 0) if transposed else (tile, layer, 0, jn)

    return pl.BlockSpec(shape, index)


def _mla_down_body(h_ref, w_ref, qn_ref, kvn_ref, cos_ref, sin_ref, *refs, rq, rkv, seq, n_ctx_tiles, n_alias):
    cq_ref, ckv_ref, kpe16_ref, st_ckv_ref, st_kpe_ref = refs[n_alias:]
    i = pl.program_id(0)
    wins = _row_windows(h_ref.shape[0], seq)
    accs = [_dot(h_ref[w, :], w_ref[...]) for w in wins]
    for w, a in zip(wins, accs):
        cq_ref[w, :] = (_rms(a[:, :rq]) * qn_ref[...]).astype(BF16)
        ckv = _rms(a[:, rq:rq + rkv]) * kvn_ref[...]
        ckv_ref[w, :] = ckv
        kpe = _rope(a[:, rq + rkv:], cos_ref[w, :], sin_ref[w, :])
        kpe16_ref[w, :] = kpe.astype(BF16)

        @pl.when(i < n_ctx_tiles)
        def _():
            _store_state(st_ckv_ref, w, seq, ckv, False)
            _store_state(st_kpe_ref, w, seq, kpe.T[:MLA_ROPE], True)


def _mla_down_call(h, w_cat, q_norm, kv_norm, cos_t, sin_t, layer, n_layers, states, g):
    m, d = h.shape
    rq, rkv = q_norm.shape[0], kv_norm.shape[0]
    n = w_cat.shape[1]
    tm = _tile(math.gcd(g.mp, g.ts), 512)
    assert tm % g.s == 0
    nct = g.mp // tm
    const = lambda i: (0, 0)
    rows = lambda i: (i, 0)
    n_alias = 0 if states is None else 2
    colb = lambda i: (i, 0)
    return pl.pallas_call(
        functools.partial(_mla_down_body, rq=rq, rkv=rkv, seq=g.s, n_ctx_tiles=nct, n_alias=n_alias),
        grid=(m // tm,),
        in_specs=[pl.BlockSpec((tm, d), rows),
                  pl.BlockSpec((d, n), const, pipeline_mode=pl.Buffered(1)),
                  pl.BlockSpec((1, rq), const), pl.BlockSpec((1, rkv), const),
                  pl.BlockSpec((tm, LANE), rows), pl.BlockSpec((tm, LANE), rows)]
                 + [pl.BlockSpec(memory_space=pl.ANY)] * n_alias,
        out_specs=[pl.BlockSpec((tm, rq), rows), pl.BlockSpec((tm, rkv), rows), pl.BlockSpec((tm, LANE), rows),
                   _state_spec(tm, g.s, nct, layer, g.s, rkv, False, colb),
                   _state_spec(tm, g.s, nct, layer, g.s, MLA_ROPE, True, colb)],
        out_shape=[jax.ShapeDtypeStruct((m, rq), BF16), jax.ShapeDtypeStruct((m, rkv), F32),
                   jax.ShapeDtypeStruct((m, LANE), BF16),
                   jax.ShapeDtypeStruct((g.b, n_layers, g.s, rkv), F32),
                   jax.ShapeDtypeStruct((g.b, n_layers, MLA_ROPE, g.s), F32)],
        input_output_aliases={6 + a: 3 + a for a in range(n_alias)},
        compiler_params=_params(("arbitrary",), d * n * 2 + 2 * tm * (d * 2 + n * 8) + tm * n * 8 + 2**23),
        name="mla_down",
    )(h, w_cat, _row_vec(q_norm), _row_vec(kv_norm), cos_t, sin_t, *(states or ()))


def _store_col_tiles(o_ref, win, qt, value_t):
    for r in range((win.stop - win.start) // qt):
        o_ref[win.start // qt + r] = value_t[:, r * qt:(r + 1) * qt].astype(o_ref.dtype)


def _rope_t(x, cos, sin):
    q = ROT_DIM // 4
    rot = jnp.concatenate([-x[q:2 * q], x[:q], -x[3 * q:], x[2 * q:3 * q]], axis=0)
    return x * cos + rot * sin


def _mla_q_body(cq_ref, wt_ref, cos_ref, sin_ref, q_ref, *, heads, scale):
    qt = q_ref.shape[2]
    wins = _row_windows(cq_ref.shape[0], qt)
    accs = [_dot_nt(wt_ref[...], cq_ref[w, :]) for w in wins]
    for w, at in zip(wins, accs):
        cos, sin = cos_ref[:ROT_DIM, w], sin_ref[:ROT_DIM, w]
        for h in range(heads):
            lo = 2 * LANE * h
            pe = _rope_t(at[lo + LANE:lo + LANE + ROT_DIM], cos, sin)
            blk = jnp.concatenate([at[lo:lo + LANE], pe, at[lo + LANE + ROT_DIM:lo + 2 * LANE]], axis=0) * scale
            _store_col_tiles(q_ref.at[:, lo:lo + 2 * LANE, :], w, qt, blk)


def _mla_q_call(cq, w_uq_pad_t, cos_tt, sin_tt, heads, g):
    m, rq = cq.shape
    n = w_uq_pad_t.shape[0]
    tm = _tile(math.gcd(g.mp, g.ts), 512, max(g.qt, LANE))
    const = lambda i: (0, 0)
    scale = (MLA_NOPE + MLA_ROPE) ** -0.5 * LOG2E
    return pl.pallas_call(
        functools.partial(_mla_q_body, heads=heads, scale=scale),
        grid=(m // tm,),
        in_specs=[pl.BlockSpec((tm, rq), lambda i: (i, 0)),
                  pl.BlockSpec((n, rq), const, pipeline_mode=pl.Buffered(1)),
                  pl.BlockSpec((LANE, tm), lambda i: (0, i)), pl.BlockSpec((LANE, tm), lambda i: (0, i))],
        out_specs=pl.BlockSpec((tm // g.qt, n, g.qt), lambda i: (i, 0, 0)),
        out_shape=jax.ShapeDtypeStruct((m // g.qt, n, g.qt), BF16),
        compiler_params=_params(("arbitrary",), rq * n * 2 + 2 * tm * (rq * 2 + n * 2) + 4 * tm * n * 4 + 2**23),
        name="mla_q_up",
    )(cq, w_uq_pad_t, cos_tt, sin_tt)


def _kv_up_body(x_ref, kpe_ref, w_ref, kcat_ref, vt_ref, *, heads):
    wins = _row_windows(x_ref.shape[0], LANE)
    accs = [_dot(x_ref[w, :].astype(BF16), w_ref[...]) for w in wins]
    for w, a in zip(wins, accs):
        kpe = kpe_ref[w, :]
        for h in range(heads):
            lo = 2 * LANE * h
            kcat_ref[w, lo:lo + LANE] = a[:, lo:lo + LANE].astype(BF16)
            kcat_ref[w, lo + LANE:lo + 2 * LANE] = kpe
            vt_ref[h * LANE:(h + 1) * LANE, w] = a[:, lo + LANE:lo + 2 * LANE].T.astype(BF16)


def _kv_up_call(x, kpe16, w_all, j, heads, name):
    m, k = x.shape
    n = w_all.shape[2]
    tm = _tile(m, 512)
    rows = lambda i: (i, 0)
    return pl.pallas_call(
        functools.partial(_kv_up_body, heads=heads),
        grid=(m // tm,),
        in_specs=[pl.BlockSpec((tm, k), rows), pl.BlockSpec((tm, LANE), rows),
                  pl.BlockSpec((None, k, n), lambda i: (j, 0, 0), pipeline_mode=pl.Buffered(1))],
        out_specs=[pl.BlockSpec((tm, 2 * heads * LANE), rows), pl.BlockSpec((heads * LANE, tm), lambda i: (0, i))],
        out_shape=[jax.ShapeDtypeStruct((m, 2 * heads * LANE), BF16), jax.ShapeDtypeStruct((heads * LANE, m), BF16)],
        compiler_params=_params(("arbitrary",), k * n * 2 + 2 * tm * (k * 4 + n * 3) + tm * n * 4 + 2**23),
        name=name,
    )(x, kpe16, w_all)


def _row_fold(x, op):
    out = x[:SUBLANE]
    for r in range(1, x.shape[0] // SUBLANE):
        out = op(out, x[r * SUBLANE:(r + 1) * SUBLANE])
    return out


def _chunks(segments):
    out = []
    for k_ref, vt_ref in segments:
        n = k_ref.shape[0]
        assert n % LANE == 0
        out += [(k_ref, vt_ref, r0, min(KV_CHUNK, n - r0)) for r0 in range(0, n, KV_CHUNK)]
    return out


def _scores_pass(qt, segments, s_ref):
    m = None
    off = 0
    for k_ref, _, r0, n in _chunks(segments):
        s = _dot(k_ref[r0:r0 + n, :], qt)
        s_ref[off:off + n, :] = s
        cm = _row_fold(s, jnp.maximum)
        m = cm if m is None else jnp.maximum(m, cm)
        off += n
    return jnp.max(m, axis=0, keepdims=True)


def _values_pass(segments, s_ref, m):
    acc = l = None
    off = 0
    for _, vt_ref, r0, n in _chunks(segments):
        e = jnp.exp2(s_ref[off:off + n, :] - m)
        cl = _row_fold(e, jnp.add)
        pv = _dot(vt_ref[:, r0:r0 + n], e.astype(BF16))
        l = cl if l is None else l + cl
        acc = pv if acc is None else acc + pv
        off += n
    return acc * (1.0 / jnp.sum(l, axis=0, keepdims=True))


def _pipelined_tiles(n_tiles, scores_of, values_of, s_a, s_b):
    if n_tiles == 1:
        values_of(0, s_a, scores_of(0, s_a))
        return
    unroll = TILE_UNROLL if n_tiles % TILE_UNROLL == 0 else 2
    assert n_tiles % unroll == 0 and unroll % 2 == 0
    bufs = (s_a, s_b)

    def body(j, m):
        t = unroll * j
        for u in range(unroll):
            nxt = t + u + 1 if u + 1 < unroll else jnp.minimum(t + unroll, n_tiles - 1)
            m_nxt = scores_of(nxt, bufs[(u + 1) % 2])
            values_of(t + u, bufs[u % 2], m)
            m = m_nxt
        return m

    lax.fori_loop(0, n_tiles // unroll, body, scores_of(0, s_a))


def _tile_rows(t, n):
    return pl.ds(t * n, n) if isinstance(t, int) else pl.ds(pl.multiple_of(t * n, n), n)


def _pipelined_static(tiles, scores_of, values_of, bufs):
    grp = len(bufs) // 2
    groups = [tiles[i:i + grp] for i in range(0, len(tiles), grp)]
    prev = None
    for gi, group in enumerate(groups):
        half = bufs[(gi % 2) * grp:(gi % 2 + 1) * grp]
        cur = [(tile, buf, scores_of(tile, buf)) for tile, buf in zip(group, half)]
        for tile, buf, m in prev or ():
            values_of(tile, buf, m)
        prev = cur
    for tile, buf, m in prev:
        values_of(tile, buf, m)


def _mla_attn_ctx_body(q_ref, k_ref, vt_ref, o_ref, *bufs, heads):
    nq = q_ref.shape[2]
    seg = lambda h: [(k_ref.at[:, 2 * LANE * h:2 * LANE * (h + 1)], vt_ref.at[LANE * h:LANE * (h + 1), :])]

    def scores_of(tile, s_ref):
        h, r = tile
        return _scores_pass(q_ref[r, 2 * LANE * h:2 * LANE * (h + 1), :], seg(h), s_ref)

    def values_of(tile, s_ref, m):
        h, r = tile
        o_ref[r * nq:(r + 1) * nq, LANE * h:LANE * (h + 1)] = _values_pass(seg(h), s_ref, m).T.astype(BF16)

    tiles = [(h, r) for h in range(heads) for r in range(q_ref.shape[0])]
    _pipelined_static(tiles, scores_of, values_of, bufs)


def _mla_attn_lat_body(q_ref, kc_ref, vc_ref, kn_ref, vn_ref, o_alias, o_ref, s_a, s_b):
    del o_alias
    segs = [(kc_ref, vc_ref), (kn_ref, vn_ref)]
    nq = q_ref.shape[2]

    def scores_of(t, s_ref):
        return _scores_pass(q_ref[t], segs, s_ref)

    def values_of(t, s_ref, m):
        o_ref[_tile_rows(t, nq), :] = _values_pass(segs, s_ref, m).T.astype(BF16)

    _pipelined_tiles(q_ref.shape[0], scores_of, values_of, s_a, s_b)


def _mla_attn(q, kcat, vt, kcat_c, vt_c, heads, g):
    m = kcat.shape[0]
    hn = heads * LANE
    past = kcat_c.shape[0] // g.bs
    o = pl.pallas_call(
        functools.partial(_mla_attn_ctx_body, heads=heads),
        grid=(g.b,),
        in_specs=[pl.BlockSpec((g.s // g.qt, 2 * hn, g.qt), lambda b: (b, 0, 0)),
                  pl.BlockSpec((g.s, 2 * hn), lambda b: (b, 0)),
                  pl.BlockSpec((hn, g.s), lambda b: (0, b))],
        out_specs=pl.BlockSpec((g.s, hn), lambda b: (b, 0)),
        out_shape=jax.ShapeDtypeStruct((m, hn), BF16),
        scratch_shapes=[pltpu.VMEM((g.s, g.qt), F32)] * (2 * CTX_GROUP),
        compiler_params=_params(("arbitrary",), 2 * g.s * hn * 12 + 2**24),
        name="mla_attn_ctx",
    )(q, kcat, vt)
    kb0 = g.mp // g.ts
    return pl.pallas_call(
        _mla_attn_lat_body,
        grid=(g.bs, heads),
        in_specs=[pl.BlockSpec((g.ts // g.qt, 2 * LANE, g.qt), lambda b, h: (kb0 + b, h, 0)),
                  pl.BlockSpec((past, 2 * LANE), lambda b, h: (b, h)),
                  pl.BlockSpec((LANE, past), lambda b, h: (h, b)),
                  pl.BlockSpec((g.ts, 2 * LANE), lambda b, h: (kb0 + b, h)),
                  pl.BlockSpec((LANE, g.ts), lambda b, h: (h, kb0 + b)),
                  pl.BlockSpec(memory_space=pl.ANY)],
        out_specs=pl.BlockSpec((g.ts, LANE), lambda b, h: (kb0 + b, h)),
        out_shape=jax.ShapeDtypeStruct((m, hn), BF16),
        input_output_aliases={5: 0},
        scratch_shapes=[pltpu.VMEM((past + g.ts, g.qt), F32)] * 2,
        compiler_params=_params(("arbitrary", "arbitrary"), 2**25),
        name="mla_attn_lat",
    )(q, kcat_c, vt_c, kcat, vt, o)


def _diff_lambda(lam_ref, lam_init):
    lv = lam_ref[...]
    s1 = jnp.sum(lv[0:1] * lv[1:2], axis=-1, keepdims=True)
    s2 = jnp.sum(lv[2:3] * lv[3:4], axis=-1, keepdims=True)
    return jnp.exp(s1) - jnp.exp(s2) + lam_init


def _diff_queries_t(qt):
    qf = qt.astype(F32)
    row = lax.broadcasted_iota(jnp.int32, qf.shape, 0)
    return jnp.concatenate([jnp.where(row < DIFF_DK, qf, 0.0), jnp.where(row < DIFF_DK, 0.0, qf)],
                           axis=1).astype(BF16)


def _diff_combine(ot, lam, subln_col, lam_init):
    n = ot.shape[1] // 2
    dt = ot[:, :n] - lam * ot[:, n:]
    dt = dt * lax.rsqrt(jnp.mean(dt * dt, axis=0, keepdims=True) + NORM_EPS) * subln_col * (1.0 - lam_init)
    return dt.T.astype(BF16)


def _diff_attn_ctx_body(q_ref, k_ref, vt_ref, lam_ref, sub_ref, o_ref, *bufs, heads, lam_init):
    lam = _diff_lambda(lam_ref, lam_init)
    nq = q_ref.shape[2]
    cols = lambda h: slice(LANE * h, LANE * (h + 1))
    seg = lambda h: [(k_ref.at[:, cols(h)], vt_ref.at[cols(h), :])]

    def scores_of(tile, s_ref):
        h, r = tile
        return _scores_pass(_diff_queries_t(q_ref[r, cols(h), :]), seg(h), s_ref)

    def values_of(tile, s_ref, m):
        h, r = tile
        o_ref[r * nq:(r + 1) * nq, cols(h)] = _diff_combine(_values_pass(seg(h), s_ref, m), lam, sub_ref[...],
                                                           lam_init)

    tiles = [(h, r) for h in range(heads) for r in range(q_ref.shape[0])]
    _pipelined_static(tiles, scores_of, values_of, bufs)


def _diff_attn_lat_body(q_ref, kc_ref, vc_ref, kn_ref, vn_ref, lam_ref, sub_ref, o_alias, o_ref, s_a, s_b, *,
                        lam_init):
    del o_alias
    lam = _diff_lambda(lam_ref, lam_init)
    segs = [(kc_ref, vc_ref), (kn_ref, vn_ref)]
    nq = q_ref.shape[2]

    def scores_of(t, s_ref):
        return _scores_pass(_diff_queries_t(q_ref[t]), segs, s_ref)

    def values_of(t, s_ref, m):
        o_ref[_tile_rows(t, nq), :] = _diff_combine(_values_pass(segs, s_ref, m), lam, sub_ref[...], lam_init)

    _pipelined_tiles(q_ref.shape[0], scores_of, values_of, s_a, s_b)


def _diff_attn(q, k, vt, k_c, vt_c, lam_vecs, subln, lam_init, heads, g):
    m = k.shape[0]
    hn = heads * LANE
    past = k_c.shape[0] // g.bs
    sub = subln.reshape(-1, 1)
    blk = lambda b: (b, 0)
    qd = g.qt // 2
    o = pl.pallas_call(
        functools.partial(_diff_attn_ctx_body, heads=heads, lam_init=lam_init),
        grid=(g.b,),
        in_specs=[pl.BlockSpec((g.s // qd, hn, qd), lambda b: (b, 0, 0)), pl.BlockSpec((g.s, hn), blk),
                  pl.BlockSpec((hn, g.s), lambda b: (0, b)),
                  pl.BlockSpec((4, DIFF_DK), lambda b: (0, 0)), pl.BlockSpec((DIFF_DV, 1), lambda b: (0, 0))],
        out_specs=pl.BlockSpec((g.s, hn), blk),
        out_shape=jax.ShapeDtypeStruct((m, hn), BF16),
        scratch_shapes=[pltpu.VMEM((g.s, g.qt), F32)] * (2 * CTX_GROUP),
        compiler_params=_params(("arbitrary",), 2 * g.s * hn * 12 + 2**24),
        name="diff_attn_ctx",
    )(q, k, vt, lam_vecs, sub)
    kb0 = g.mp // g.ts
    return pl.pallas_call(
        functools.partial(_diff_attn_lat_body, lam_init=lam_init),
        grid=(g.bs, heads),
        in_specs=[pl.BlockSpec((g.ts // qd, LANE, qd), lambda b, h: (kb0 + b, h, 0)),
                  pl.BlockSpec((past, LANE), lambda b, h: (b, h)),
                  pl.BlockSpec((LANE, past), lambda b, h: (h, b)),
                  pl.BlockSpec((g.ts, LANE), lambda b, h: (kb0 + b, h)),
                  pl.BlockSpec((LANE, g.ts), lambda b, h: (h, kb0 + b)),
                  pl.BlockSpec((4, DIFF_DK), lambda b, h: (0, 0)),
                  pl.BlockSpec((DIFF_DV, 1), lambda b, h: (0, 0)),
                  pl.BlockSpec(memory_space=pl.ANY)],
        out_specs=pl.BlockSpec((g.ts, LANE), lambda b, h: (kb0 + b, h)),
        out_shape=jax.ShapeDtypeStruct((m, hn), BF16),
        input_output_aliases={7: 0},
        scratch_shapes=[pltpu.VMEM((past + g.ts, g.qt), F32)] * 2,
        compiler_params=_params(("arbitrary", "arbitrary"), 2**25),
        name="diff_attn_lat",
    )(q, k_c, vt_c, k, vt, lam_vecs, sub, o)


def _proj_body(h_ref, wt_ref, cos_ref, sin_ref, *refs, rope, scale, kinds, seq, n_ctx_tiles, n_alias):
    out_refs = refs[n_alias:]
    i = pl.program_id(1)
    wins = _row_windows(h_ref.shape[0], max(seq, LANE))
    accs = [_dot_nt(wt_ref[...], h_ref[w, :]) for w in wins]
    for w, at in zip(wins, accs):
        if rope:
            cos, sin = cos_ref[:ROT_DIM, w], sin_ref[:ROT_DIM, w]
            at = jnp.concatenate([_rope_t(at[r:r + ROT_DIM], cos, sin) for r in range(0, at.shape[0], ROT_DIM)],
                                 axis=0)
        if scale != 1.0:
            at = at * scale
        for o_ref, kind in zip(out_refs, kinds):
            if kind == "rows":
                o_ref[w, :] = at.T.astype(o_ref.dtype)
            elif kind == "cols":
                o_ref[:, w] = at.astype(o_ref.dtype)
            elif kind == "col_tiles":
                _store_col_tiles(o_ref, w, o_ref.shape[2], at)
            else:
                @pl.when(i < n_ctx_tiles)
                def _(o_ref=o_ref, kind=kind, at=at, w=w):
                    _store_state(o_ref, w, seq, at if kind == "state_cols" else at.T, kind == "state_cols")


def _proj_call(h, wt_all, layer, col0, ncols, cos_tt, sin_tt, rope, scale, outs, st_layer, st_layers, state, g, name):
    m, d = h.shape
    tn = _tile(ncols, 1024, LANE)
    tm = _tile(math.gcd(g.mp, g.ts), 512, LANE)
    assert tm % g.s == 0
    nct = g.mp // tm
    nb0 = col0 // tn
    colb = lambda j, i: (i, j)
    specs, shapes = [], []
    for kind, dt in outs:
        if kind == "rows":
            specs.append(pl.BlockSpec((tm, tn), lambda j, i: (i, j)))
            shapes.append(jax.ShapeDtypeStruct((m, ncols), dt))
        elif kind == "cols":
            specs.append(pl.BlockSpec((tn, tm), lambda j, i: (j, i)))
            shapes.append(jax.ShapeDtypeStruct((ncols, m), dt))
        elif kind == "col_tiles":
            qd = g.qt // 2
            specs.append(pl.BlockSpec((tm // qd, tn, qd), lambda j, i: (i, j, 0)))
            shapes.append(jax.ShapeDtypeStruct((m // qd, ncols, qd), dt))
        else:
            tr = kind == "state_cols"
            specs.append(_state_spec(tm, g.s, nct, st_layer, g.s, tn, tr, colb))
            shapes.append(jax.ShapeDtypeStruct((g.b, st_layers, ncols, g.s) if tr else (g.b, st_layers, g.s, ncols), dt))
    st_idx = [k for k, (kind, _) in enumerate(outs) if kind.startswith("state")]
    n_alias = len(st_idx) if state is not None else 0
    return pl.pallas_call(
        functools.partial(_proj_body, rope=rope, scale=scale, kinds=tuple(k for k, _ in outs), seq=g.s,
                          n_ctx_tiles=nct, n_alias=n_alias),
        grid=(ncols // tn, m // tm),
        in_specs=[pl.BlockSpec((tm, d), lambda j, i: (i, 0)),
                  pl.BlockSpec((None, tn, d), lambda j, i: (layer, nb0 + j, 0)),
                  pl.BlockSpec((LANE, tm), lambda j, i: (0, i)), pl.BlockSpec((LANE, tm), lambda j, i: (0, i))]
                 + [pl.BlockSpec(memory_space=pl.ANY)] * n_alias,
        out_specs=specs,
        out_shape=shapes,
        input_output_aliases={4 + a: st_idx[a] for a in range(n_alias)},
        compiler_params=_params(("arbitrary", "arbitrary"),
                                d * tn * 4 + 2 * tm * (d * 2 + tn * 6) + 4 * tm * tn * 4 + 2**23),
        name=name,
    )(h, wt_all, cos_tt, sin_tt, *([state] if n_alias else []))


def _out_ln_body(o_ref, w_ref, x_ref, gate_ref, g_ref, b_ref, sc_ref, sh_ref, xo_ref, ho_ref, *, alpha):
    wins = _row_windows(o_ref.shape[0], 16)
    accs = [_dot(o_ref[w, :], w_ref[...]) for w in wins]
    for w, y in zip(wins, accs):
        xn = _layer_norm(alpha * x_ref[w, :] + gate_ref[...] * y, g_ref[...], b_ref[...])
        xo_ref[w, :] = xn
        ho_ref[w, :] = (xn * (1.0 + sc_ref[...]) + sh_ref[...]).astype(BF16)


def _out_ln_call(o, w_o_all, layer, x, mod, ln_g, ln_b, alpha, g):
    m, k = o.shape
    d = w_o_all.shape[2]
    tm = _tile(math.gcd(g.mp, g.ts), 512)
    row = lambda i: _mod_row(i, tm, g.mp, g.ts)
    const = lambda i: (0, 0)
    rows = lambda i: (i, 0)
    return pl.pallas_call(
        functools.partial(_out_ln_body, alpha=alpha),
        grid=(m // tm,),
        in_specs=[pl.BlockSpec((tm, k), rows),
                  pl.BlockSpec((None, k, d), lambda i: (layer, 0, 0), pipeline_mode=pl.Buffered(1)),
                  pl.BlockSpec((tm, d), rows), _mod_spec(2, d, row),
                  pl.BlockSpec((1, d), const), pl.BlockSpec((1, d), const),
                  _mod_spec(4, d, row), _mod_spec(3, d, row)],
        out_specs=[pl.BlockSpec((tm, d), rows), pl.BlockSpec((tm, d), rows)],
        out_shape=[jax.ShapeDtypeStruct((m, d), F32), jax.ShapeDtypeStruct((m, d), BF16)],
        compiler_params=_params(("arbitrary",), k * d * 2 + 2 * tm * (k * 2 + d * 10) + 4 * tm * d * 4 + 2**23),
        name="out_proj_ln",
    )(o, w_o_all, x, mod, _row_vec(ln_g), _row_vec(ln_b), mod, mod)


def _lane_cumsum(x, tri):
    c = tri.shape[0]
    carry = jnp.zeros((x.shape[0], 1), F32)
    parts = []
    for j in range(x.shape[1] // c):
        xc = x[:, j * c:(j + 1) * c]
        parts.append(_dot(xc.astype(BF16), tri) + carry)
        carry = carry + jnp.sum(xc, axis=1, keepdims=True)
    return parts[0] if len(parts) == 1 else jnp.concatenate(parts, axis=1)


def _route_body(h_ref, wr_ref, slot_ref, gate_ref, *, groups, t, cap):
    logits = _dot_nt(wr_ref[...].astype(BF16), h_ref[...])
    ex = jnp.exp(logits - jnp.max(logits, axis=0, keepdims=True))
    aff = ex / jnp.sum(ex, axis=0, keepdims=True)
    nbits = jnp.finfo(aff.dtype).bits
    keys = lax.bitcast_convert_type(aff, jnp.dtype(f"int{nbits}"))
    c = min(t, 2 * LANE)
    tri = jnp.where(lax.broadcasted_iota(jnp.int32, (c, c), 0) <= lax.broadcasted_iota(jnp.int32, (c, c), 1),
                    1.0, 0.0).astype(BF16)
    capf = float(cap)
    for gi in range(groups):
        k = keys[:, gi * t:(gi + 1) * t]
        thr = jnp.zeros((k.shape[0], 1), keys.dtype)
        for bit in range(nbits - 2, -1, -1):
            cand = thr | (1 << bit)
            cnt = jnp.sum(jnp.where(k >= cand, 1.0, 0.0), axis=1, keepdims=True)
            thr = jnp.where(cnt >= capf, cand, thr)
        gt = jnp.where(k > thr, 1.0, 0.0)
        eq = jnp.where(k == thr, 1.0, 0.0)
        need = capf - jnp.sum(gt, axis=1, keepdims=True)
        eq_before = _lane_cumsum(eq, tri) - eq
        sel = gt + eq * jnp.where(eq_before < need, 1.0, 0.0)
        pos = _lane_cumsum(sel, tri) - 1.0
        slot_ref[gi] = jnp.where(sel > 0.5, pos, -1.0).astype(jnp.int32)
        gate_ref[gi] = aff[:, gi * t:(gi + 1) * t]


def _route_call(h2, w_router_t, row0, nreq, t, cap, groups, name):
    e, d = w_router_t.shape
    rb0 = row0 // (groups * t)
    return pl.pallas_call(
        functools.partial(_route_body, groups=groups, t=t, cap=cap),
        grid=(nreq // groups,),
        in_specs=[pl.BlockSpec((groups * t, d), lambda i: (rb0 + i, 0)), pl.BlockSpec((e, d), lambda i: (0, 0))],
        out_specs=[pl.BlockSpec((groups, e, t), lambda i: (i, 0, 0)), pl.BlockSpec((groups, e, t), lambda i: (i, 0, 0))],
        out_shape=[jax.ShapeDtypeStruct((nreq, e, t), jnp.int32), jax.ShapeDtypeStruct((nreq, e, t), F32)],
        compiler_params=_params(("parallel",), 2 * groups * t * d * 2 + 2**24),
        name=name,
    )(h2, w_router_t)


def _one_hot_rows(slot_row, cap):
    r = lax.broadcasted_iota(jnp.int32, (cap, slot_row.shape[1]), 0)
    return slot_row == r


def _gather_ctx_body(h_ref, slot_ref, gate_ref, xs_ref, gs_ref, *, cap):
    ne = slot_ref.shape[0]
    masks = [_one_hot_rows(slot_ref[e:e + 1, :], cap) for e in range(ne)]
    p = jnp.concatenate([jnp.where(mk, 1.0, 0.0) for mk in masks], axis=0).astype(BF16)
    xs = _dot(p, h_ref[...]).astype(BF16)
    for e in range(ne):
        xs_ref[e] = xs[e * cap:(e + 1) * cap]
        gs_ref[e] = jnp.sum(jnp.where(masks[e], gate_ref[e:e + 1, :], 0.0), axis=1, keepdims=True)


def _gather_lat_body(h_ref, slot_ref, gate_ref, xs_alias, gs_alias, xs_ref, gs_ref, *, cap):
    del xs_alias, gs_alias
    e = pl.program_id(1)
    mk = _one_hot_rows(slot_ref[pl.ds(e, 1), :], cap)
    xs_ref[...] = _dot(jnp.where(mk, 1.0, 0.0).astype(BF16), h_ref[...]).astype(BF16)
    gs_ref[...] = jnp.sum(jnp.where(mk, gate_ref[pl.ds(e, 1), :], 0.0), axis=1, keepdims=True)


def _gather(h2, slot_p, gate_p, slot_s, gate_s, g):
    d = h2.shape[1]
    ne = slot_p.shape[1]
    rows_e = g.b * g.cap_p + g.bs * g.cap_s
    shapes = [jax.ShapeDtypeStruct((ne, rows_e, d), BF16), jax.ShapeDtypeStruct((ne, rows_e, 1), F32)]
    xs, gs = pl.pallas_call(
        functools.partial(_gather_ctx_body, cap=g.cap_p),
        grid=(g.b,),
        in_specs=[pl.BlockSpec((g.s, d), lambda b: (b, 0)),
                  pl.BlockSpec((None, ne, g.s), lambda b: (b, 0, 0)), pl.BlockSpec((None, ne, g.s), lambda b: (b, 0, 0))],
        out_specs=[pl.BlockSpec((ne, g.cap_p, d), lambda b: (0, b, 0)), pl.BlockSpec((ne, g.cap_p, 1), lambda b: (0, b, 0))],
        out_shape=shapes,
        compiler_params=_params(("parallel",), 2 * ne * g.cap_p * (d * 8 + g.s * 8) + 2**24),
        name="moe_gather_ctx",
    )(h2, slot_p, gate_p)
    blk0 = (g.b * g.cap_p) // g.cap_s
    hb0 = g.mp // g.ts
    return pl.pallas_call(
        functools.partial(_gather_lat_body, cap=g.cap_s),
        grid=(g.bs, ne),
        in_specs=[pl.BlockSpec((g.ts, d), lambda b, e: (hb0 + b, 0)),
                  pl.BlockSpec((None, ne, g.ts), lambda b, e: (b, 0, 0)), pl.BlockSpec((None, ne, g.ts), lambda b, e: (b, 0, 0)),
                  pl.BlockSpec(memory_space=pl.ANY), pl.BlockSpec(memory_space=pl.ANY)],
        out_specs=[pl.BlockSpec((None, g.cap_s, d), lambda b, e: (e, blk0 + b, 0)),
                   pl.BlockSpec((None, g.cap_s, 1), lambda b, e: (e, blk0 + b, 0))],
        out_shape=shapes,
        input_output_aliases={3: 0, 4: 1},
        compiler_params=_params(("parallel", "parallel"), 2 * g.ts * d * 2 + g.cap_s * (g.ts * 12 + d * 12) + 2**24),
        name="moe_gather_lat",
    )(h2, slot_s, gate_s, xs, gs)


def _ffn1_body(x_ref, wg_ref, wu_ref, o_ref):
    x = x_ref[...]
    a = _dot(x, wg_ref[...].astype(BF16))
    u = _dot(x, wu_ref[...].astype(BF16))
    o_ref[...] = (a * _sigmoid(a) * u).astype(BF16)


def _ffn1_call(xs, w_gate, w_up, layer):
    ne, rows, d = xs.shape
    f = w_gate.shape[3]
    tn = _tile(f, 512, LANE)
    wspec = pl.BlockSpec((None, None, d, tn), lambda e, n: (layer, e, 0, n))
    return pl.pallas_call(
        _ffn1_body,
        grid=(ne, f // tn),
        in_specs=[pl.BlockSpec((None, rows, d), lambda e, n: (e, 0, 0)), wspec, wspec],
        out_specs=pl.BlockSpec((None, rows, tn), lambda e, n: (e, 0, n)),
        out_shape=jax.ShapeDtypeStruct((ne, rows, f), BF16),
        compiler_params=_params(("parallel", "parallel"), 2 * rows * d * 2 + 4 * d * tn * 4 + 2 * d * tn * 2 + 5 * rows * tn * 4 + 2**23),
        name="moe_ffn_gate_up",
    )(xs, w_gate, w_up)


def _ffn2_body(h_ref, w_ref, gs_ref, o_ref):
    o_ref[...] = (_dot(h_ref[...], w_ref[...].astype(BF16)) * gs_ref[...]).astype(BF16)


def _ffn2_call(hm, w_down, gs, layer):
    ne, rows, f = hm.shape
    d = w_down.shape[3]
    tn = _tile(d, 512, LANE)
    return pl.pallas_call(
        _ffn2_body,
        grid=(ne, d // tn),
        in_specs=[pl.BlockSpec((None, rows, f), lambda e, n: (e, 0, 0)),
                  pl.BlockSpec((None, None, f, tn), lambda e, n: (layer, e, 0, n)),
                  pl.BlockSpec((None, rows, 1), lambda e, n: (e, 0, 0))],
        out_specs=pl.BlockSpec((None, rows, tn), lambda e, n: (e, 0, n)),
        out_shape=jax.ShapeDtypeStruct((ne, rows, d), BF16),
        compiler_params=_params(("parallel", "parallel"), 2 * rows * f * 2 + 2 * f * tn * 4 + f * tn * 2 + 3 * rows * tn * 4 + 2 * rows * LANE * 4 + 2**23),
        name="moe_ffn_down",
    )(hm, w_down, gs)


def _combine(y_ref, slot, cap):
    ne = y_ref.shape[0]
    p = jnp.concatenate([jnp.where(_one_hot_rows(slot[e:e + 1, :], cap), 1.0, 0.0) for e in range(ne)], axis=0)
    y = y_ref[...].reshape(ne * cap, y_ref.shape[2])
    return _dot_tn(p.astype(BF16), y)


def _combine_ln_body(*refs, cap, alpha, n_alias, emit_h):
    y_ref, slot_ref, x_ref, gate_ref, g_ref, b_ref = refs[:6]
    rest = refs[6:]
    if emit_h:
        sc_ref, sh_ref = rest[:2]
        rest = rest[2:]
    outs = rest[n_alias:]
    moe = _combine(y_ref, slot_ref[...], cap)
    xn = _layer_norm(alpha * x_ref[...] + gate_ref[...] * moe, g_ref[...], b_ref[...])
    outs[0][...] = xn
    if emit_h:
        outs[1][...] = (xn * (1.0 + sc_ref[...]) + sh_ref[...]).astype(BF16)


def _combine_ln(y, slot_p, slot_s, x, mod, mod_next, ln_g, ln_b, alpha, g):
    ne, rows_e, d = y.shape
    m = x.shape[0]
    emit_h = mod_next is not None
    lg, lb = _row_vec(ln_g), _row_vec(ln_b)
    shapes = [jax.ShapeDtypeStruct((m, d), F32), jax.ShapeDtypeStruct((m, d), BF16)] if emit_h else \
        [jax.ShapeDtypeStruct((g.mp, d), F32)]
    n_out = len(shapes)

    def specs(row):
        sp = [pl.BlockSpec((1, d), lambda *a: (0, 0)), pl.BlockSpec((1, d), lambda *a: (0, 0))]
        md = [_mod_spec(5, d, row)]
        nx = [_mod_spec(1, d, row), _mod_spec(0, d, row)] if emit_h else []
        return md, sp, nx

    md, sp, nx = specs(lambda b: 0)
    outs = pl.pallas_call(
        functools.partial(_combine_ln_body, cap=g.cap_p, alpha=alpha, n_alias=0, emit_h=emit_h),
        grid=(g.b,),
        in_specs=[pl.BlockSpec((ne, g.cap_p, d), lambda b: (0, b, 0)),
                  pl.BlockSpec((None, ne, g.s), lambda b: (b, 0, 0)),
                  pl.BlockSpec((g.s, d), lambda b: (b, 0))] + md + sp + nx,
        out_specs=[pl.BlockSpec((g.s, d), lambda b: (b, 0))] * n_out,
        out_shape=shapes,
        compiler_params=_params(("parallel",), 2 * ne * g.cap_p * (d * 2 + g.s * 8) + g.s * d * 40 + 2**24),
        name="moe_combine_ln_ctx",
    )(y, slot_p, x, mod, lg, lb, *([mod_next, mod_next] if emit_h else []))
    outs = list(outs) if isinstance(outs, (list, tuple)) else [outs]
    tt = _tile(g.ts, 256, LANE)
    nt = g.ts // tt
    blk0 = (g.b * g.cap_p) // g.cap_s
    xb0 = g.mp // tt
    md, sp, nx = specs(lambda b, i: 1 + b)
    n_in = 6 + len(nx)
    n_alias = n_out if emit_h else 0
    ob0 = xb0 if emit_h else 0
    outs_s = pl.pallas_call(
        functools.partial(_combine_ln_body, cap=g.cap_s, alpha=alpha, n_alias=n_alias, emit_h=emit_h),
        grid=(g.bs, nt),
        in_specs=[pl.BlockSpec((ne, g.cap_s, d), lambda b, i: (0, blk0 + b, 0), pipeline_mode=pl.Buffered(1)),
                  pl.BlockSpec((None, ne, tt), lambda b, i: (b, 0, i)),
                  pl.BlockSpec((tt, d), lambda b, i: (xb0 + b * nt + i, 0))] + md + sp + nx
                 + [pl.BlockSpec(memory_space=pl.ANY)] * n_alias,
        out_specs=[pl.BlockSpec((tt, d), lambda b, i: (ob0 + b * nt + i, 0))] * n_out,
        out_shape=shapes if emit_h else [jax.ShapeDtypeStruct((g.ms, d), F32)],
        input_output_aliases={n_in + j: j for j in range(n_alias)},
        compiler_params=_params(("parallel", "parallel"), ne * g.cap_s * (d * 2 + tt * 10) + tt * d * 48 + 2**24),
        name="moe_combine_ln_lat",
    )(y, slot_s, x, mod, lg, lb, *([mod_next, mod_next] if emit_h else []), *(outs if emit_h else []))
    outs_s = list(outs_s) if isinstance(outs_s, (list, tuple)) else [outs_s]
    if emit_h:
        return outs_s[0], outs_s[1]
    return outs[0], outs_s[0]


class _Geom:
    def __init__(self, b, s, bs, ts, ne):
        self.b, self.s, self.bs, self.ts = b, s, bs, ts
        self.mp, self.ms = b * s, bs * ts
        self.cap_p = CAP_FACTOR * s // ne
        self.cap_s = CAP_FACTOR * ts // ne
        self.qt = min(Q_TILE, s)
        assert ts % s == 0 and self.mp % ts == 0 and bs + 1 <= MOD_ROWS
        assert (b * self.cap_p) % self.cap_s == 0 and self.cap_p % 16 == 0


def _rope_tables(g):
    rows = g.ts // GRID_W
    row = jnp.repeat(jnp.arange(rows, dtype=F32), GRID_W)
    col = jnp.tile(jnp.arange(GRID_W, dtype=F32), rows)
    half = ROT_DIM // 2
    inv_freq = ROPE_BASE ** (-jnp.arange(0, half, 2, dtype=F32) / half)
    ang_r = row[:, None] * inv_freq
    ang_c = col[:, None] * inv_freq
    ang = jnp.concatenate([ang_r, ang_r, ang_c, ang_c] * (LANE // ROT_DIM), axis=-1)
    cos = jnp.concatenate([jnp.ones((g.mp, LANE), F32), jnp.tile(jnp.cos(ang), (g.bs, 1))], axis=0)
    sin = jnp.concatenate([jnp.zeros((g.mp, LANE), F32), jnp.tile(jnp.sin(ang), (g.bs, 1))], axis=0)
    return cos, sin


def _pad_heads(w, heads, width):
    k = w.shape[0]
    per = w.shape[1] // heads
    return jnp.pad(w.reshape(k, heads, per), ((0, 0), (0, 0), (0, width - per))).reshape(k, heads * width)


def kernel(x_prompt, x_sample, cache_mla_ckv, cache_mla_kpe, cache_diff_k, cache_diff_v, c, c_ctx, ada_w, ada_b, ln1_g, ln1_b, ln2_g, ln2_b, mla_w_dq, mla_q_norm, mla_w_uq, mla_w_dkv, mla_kv_norm, mla_w_ukv, mla_w_o, diff_w_qkv, diff_lambda_q1, diff_lambda_k1, diff_lambda_q2, diff_lambda_k2, diff_subln, diff_w_o, moe_w_router, moe_w_gate, moe_w_up, moe_w_down):
    b, s, d = x_prompt.shape
    bs, ts, _ = x_sample.shape
    depth = ada_w.shape[0]
    ne = moe_w_router.shape[-1]
    g = _Geom(b, s, bs, ts, ne)
    mla_heads = mla_w_o.shape[1] // MLA_V
    diff_heads = diff_w_o.shape[1] // DIFF_DV
    alpha = (2.0 * depth) ** 0.25
    past = cache_mla_ckv.shape[2]

    x = jnp.concatenate([x_prompt.reshape(g.mp, d), x_sample.reshape(g.ms, d)], axis=0)
    cv = jnp.concatenate([c_ctx[None, :], c, jnp.zeros((MOD_ROWS - 1 - bs, d), F32)], axis=0)
    mods = _mod_call(cv, ada_w, ada_b)
    mod_of = lambda l: mods[l].reshape(MOD_ROWS, 1, 6 * d)
    cos_t, sin_t = _rope_tables(g)
    h = _modcast_call(x, mod_of(0), g)

    n_mla, n_diff = mla_w_o.shape[0], diff_w_o.shape[0]
    mla_w_ukv, mla_w_o, diff_w_o = (w.astype(BF16) for w in (mla_w_ukv, mla_w_o, diff_w_o))
    diff_wt_qkv = jnp.swapaxes(diff_w_qkv, 1, 2).astype(BF16)
    cos_tt, sin_tt = cos_t.T, sin_t.T
    st_mla = st_dk = st_dv = None
    for l in range(depth):
        mod = mod_of(l)
        j = l // 2
        if l % 2 == 0:
            w_down = jnp.pad(jnp.concatenate([mla_w_dq[j], mla_w_dkv[j]], axis=1),
                             ((0, 0), (0, LANE - MLA_ROPE))).astype(BF16)
            cq, ckv, kpe16, *st_mla = _mla_down_call(h, w_down, mla_q_norm[j], mla_kv_norm[j], cos_t, sin_t,
                                                     j, n_mla, st_mla, g)
            w_uq_t = _pad_heads(mla_w_uq[j], mla_heads, 2 * LANE).T.astype(BF16)
            q = _mla_q_call(cq, w_uq_t, cos_tt, sin_tt, mla_heads, g)
            kpe_c = jnp.pad(cache_mla_kpe[:, j], ((0, 0), (0, 0), (0, LANE - MLA_ROPE))).astype(BF16)
            kcat, vt = _kv_up_call(ckv, kpe16, mla_w_ukv, j, mla_heads, "mla_kv_up")
            kcat_c, vt_c = _kv_up_call(cache_mla_ckv[:, j].reshape(bs * past, -1), kpe_c.reshape(bs * past, LANE),
                                       mla_w_ukv, j, mla_heads, "mla_kv_up_cache")
            o = _mla_attn(q, kcat, vt, kcat_c, vt_c, mla_heads, g)
            w_o = mla_w_o
        else:
            lam_init = 0.8 - 0.6 * math.exp(-0.3 * l)
            nq = diff_heads * 2 * DIFF_DK
            nv = diff_heads * DIFF_DV
            q, = _proj_call(h, diff_wt_qkv, j, 0, nq, cos_tt, sin_tt, True, DIFF_DK ** -0.5 * LOG2E,
                            [("col_tiles", BF16)], j, n_diff, None, g, "diff_q")
            k16, st_dk = _proj_call(h, diff_wt_qkv, j, nq, nq, cos_tt, sin_tt, True, 1.0,
                                    [("rows", BF16), ("state_cols", F32)], j, n_diff, st_dk, g, "diff_k")
            vt16, st_dv = _proj_call(h, diff_wt_qkv, j, 2 * nq, nv, cos_tt, sin_tt, False, 1.0,
                                     [("cols", BF16), ("state_rows", F32)], j, n_diff, st_dv, g, "diff_v")
            k_c = cache_diff_k[:, j].reshape(bs * past, nq).astype(BF16)
            vt_c = cache_diff_v[:, j].reshape(bs * past, nv).T.astype(BF16)
            lam_vecs = jnp.stack([diff_lambda_q1[j], diff_lambda_k1[j], diff_lambda_q2[j], diff_lambda_k2[j]])
            o = _diff_attn(q, k16, vt16, k_c, vt_c, lam_vecs, diff_subln[j], lam_init, diff_heads, g)
            w_o = diff_w_o
        x, h2 = _out_ln_call(o, w_o, j, x, mod, ln1_g[l], ln1_b[l], alpha, g)

        w_rt = moe_w_router[l].T
        groups = max(1, min(g.b, g.ts // g.s))
        slot_p, gate_p = _route_call(h2, w_rt, 0, g.b, g.s, g.cap_p, groups, "moe_route_ctx")
        slot_s, gate_s = _route_call(h2, w_rt, g.mp, g.bs, g.ts, g.cap_s, 1, "moe_route_lat")
        xs, gs = _gather(h2, slot_p, gate_p, slot_s, gate_s, g)
        hm = _ffn1_call(xs, moe_w_gate, moe_w_up, l)
        y = _ffn2_call(hm, moe_w_down, gs, l)
        mod_next = mod_of(l + 1) if l + 1 < depth else None
        x, h = _combine_ln(y, slot_p, slot_s, x, mod, mod_next, ln2_g[l], ln2_b[l], alpha, g)

    st_ckv, st_kpe = st_mla
    return (x.reshape(b, s, d), h.reshape(bs, ts, d), st_ckv, jnp.swapaxes(st_kpe, 2, 3),
            jnp.moveaxis(st_dk.reshape(b, n_diff, diff_heads, 2, DIFF_DK, s), 5, 2),
            st_dv.reshape(b, n_diff, s, diff_heads, DIFF_DV))
```

```python
import functools
import math

import jax
import jax.numpy as jnp
from jax import lax
from jax.experimental import pallas as pl
from jax.experimental.pallas import tpu as pltpu

F32 = jnp.float32
BF16 = jnp.bfloat16

NORM_EPS = 1e-6
ROPE_BASE = 10000.0
GRID_W = 64
CAP_FACTOR = 2
MLA_NOPE, MLA_ROPE, MLA_V = 128, 64, 128
DIFF_DK, DIFF_DV = 64, 128
ROT_DIM = 64
LANE = 128
MOD_ROWS = 8
LOG2E = math.log2(math.e)
SUBLANE = 8
Q_TILE = 256
KV_CHUNK = 256
ROW_SPLIT = 2
TILE_UNROLL = 4
CTX_GROUP = 4
V7X_VMEM_BYTES = 64 * 2**20
VMEM_HEADROOM_BYTES = 6 * 2**20


def _params(semantics, vmem_bytes):
    limit = min(int(vmem_bytes), V7X_VMEM_BYTES - VMEM_HEADROOM_BYTES)
    return pltpu.CompilerParams(dimension_semantics=semantics, vmem_limit_bytes=limit)


def _tile(n, target, mult=16):
    t = min(n, target)
    while t > mult and (n % t or t % mult):
        t -= mult
    assert n % t == 0 and t % mult == 0, (n, target, mult)
    return t


def _dot(a, b):
    return jnp.dot(a, b, preferred_element_type=F32)


def _dot_nt(a, b):
    return lax.dot_general(a, b, (((1,), (1,)), ((), ())), preferred_element_type=F32)


def _dot_tn(a, b):
    return lax.dot_general(a, b, (((0,), (0,)), ((), ())), preferred_element_type=F32)


def _sigmoid(x):
    return 1.0 / (1.0 + jnp.exp(-x))


def _rms(x):
    return x * lax.rsqrt(jnp.mean(x * x, axis=-1, keepdims=True) + NORM_EPS)


def _layer_norm(z, g, b):
    mu = jnp.mean(z, axis=-1, keepdims=True)
    zc = z - mu
    var = jnp.mean(zc * zc, axis=-1, keepdims=True)
    return zc * lax.rsqrt(var + NORM_EPS) * g + b


def _rope(x, cos, sin):
    n = x.shape[-1]
    nxt = pltpu.roll(x, n - 16, 1)
    prv = pltpu.roll(x, 16, 1)
    lane = lax.broadcasted_iota(jnp.int32, x.shape, 1)
    rot = jnp.where((lane % 32) < 16, -nxt, prv)
    return x * cos + rot * sin


def _mod_row(i, tm, mp, ts):
    r0 = i * tm
    return jnp.where(r0 < mp, 0, 1 + (r0 - mp) // ts)


def _mod_spec(which, d, row_fn):
    return pl.BlockSpec((None, 1, d), lambda *ids: (row_fn(*ids), 0, which))


def _row_vec(v):
    return v.reshape(1, -1)


def _mod_body(c_ref, w_ref, b_ref, o_ref):
    c = c_ref[...]
    s = (c * _sigmoid(c)).astype(BF16)
    o_ref[...] = _dot(s, w_ref[...].astype(BF16)) + b_ref[...]


def _mod_call(cv, ada_w, ada_b):
    nl, d, n6 = ada_w.shape
    tn = _tile(n6, 1024, LANE)
    return pl.pallas_call(
        _mod_body,
        grid=(nl, n6 // tn),
        in_specs=[pl.BlockSpec((MOD_ROWS, d), lambda l, n: (0, 0)),
                  pl.BlockSpec((None, d, tn), lambda l, n: (l, 0, n)),
                  pl.BlockSpec((None, 1, tn), lambda l, n: (l, 0, n))],
        out_specs=pl.BlockSpec((None, MOD_ROWS, tn), lambda l, n: (l, 0, n)),
        out_shape=jax.ShapeDtypeStruct((nl, MOD_ROWS, n6), F32),
        compiler_params=_params(("parallel", "parallel"), 4 * d * tn * 4 + 2**22),
        name="ada_mod",
    )(cv, ada_w, ada_b.reshape(nl, 1, n6))


def _modcast_body(xp_ref, xs_ref, sc_ref, sh_ref, x_ref, h_ref, *, n_ctx_tiles):
    def emit(src_ref):
        x = src_ref[...]
        x_ref[...] = x
        h_ref[...] = (x * (1.0 + sc_ref[...]) + sh_ref[...]).astype(BF16)

    i = pl.program_id(0)
    pl.when(i < n_ctx_tiles)(lambda: emit(xp_ref))
    pl.when(i >= n_ctx_tiles)(lambda: emit(xs_ref))


def _modcast_call(xp, xs, mod, g):
    d = xp.shape[1]
    m = g.mp + g.ms
    tm = _tile(math.gcd(g.mp, g.ts), 512)
    nct = g.mp // tm
    row = lambda i: _mod_row(i, tm, g.mp, g.ts)
    rows = lambda i: (i, 0)
    return pl.pallas_call(
        functools.partial(_modcast_body, n_ctx_tiles=nct),
        grid=(m // tm,),
        in_specs=[pl.BlockSpec((tm, d), lambda i: (jnp.minimum(i, nct - 1), 0)),
                  pl.BlockSpec((tm, d), lambda i: (jnp.maximum(i - nct, 0), 0)),
                  _mod_spec(1, d, row), _mod_spec(0, d, row)],
        out_specs=[pl.BlockSpec((tm, d), rows), pl.BlockSpec((tm, d), rows)],
        out_shape=[jax.ShapeDtypeStruct((m, d), F32), jax.ShapeDtypeStruct((m, d), BF16)],
        compiler_params=_params(("arbitrary",), 2 * tm * d * 14 + 2**22),
        name="modulate_in",
    )(xp, xs, mod, mod)


def _row_windows(tm, unit):
    n = ROW_SPLIT if tm % (ROW_SPLIT * unit) == 0 else 1
    return [slice(r, r + tm // n) for r in range(0, tm, tm // n)]


def _store_state(st_ref, win, seq, value, transposed):
    for r in range((win.stop - win.start) // seq):
        tok = slice(r * seq, (r + 1) * seq)
        st_ref[win.start // seq + r] = value[:, tok] if transposed else value[tok]


def _state_spec(tm, seq, n_ctx_tiles, layer, rows, cols, transposed, col_block):
    shape = (tm // seq, None, cols, rows) if transposed else (tm // seq, None, rows, cols)

    def index(*ids):
        i, jn = col_block(*ids)
        tile = jnp.minimum(i, n_ctx_tiles - 1)
        return (tile, layer, jn, 0) if transposed else (tile, layer, 0, jn)

    return pl.BlockSpec(shape, index)


def _mla_down_body(h_ref, w_ref, qn_ref, kvn_ref, cos_ref, sin_ref, *refs, rq, rkv, seq, n_ctx_tiles, n_alias):
    cq_ref, ckv_ref, kpe16_ref, st_ckv_ref, st_kpe_ref = refs[n_alias:]
    i = pl.program_id(0)
    wins = _row_windows(h_ref.shape[0], seq)
    accs = [_dot(h_ref[w, :], w_ref[...]) for w in wins]
    for w, a in zip(wins, accs):
        cq_ref[w, :] = (_rms(a[:, :rq]) * qn_ref[...]).astype(BF16)
        ckv = _rms(a[:, rq:rq + rkv]) * kvn_ref[...]
        ckv_ref[w, :] = ckv
        kpe = _rope(a[:, rq + rkv:], cos_ref[w, :], sin_ref[w, :])
        kpe16_ref[w, :] = kpe.astype(BF16)

        @pl.when(i < n_ctx_tiles)
        def _():
            _store_state(st_ckv_ref, w, seq, ckv, False)
            _store_state(st_kpe_ref, w, seq, kpe.T[:MLA_ROPE], True)


def _mla_down_call(h, w_cat, q_norm, kv_norm, cos_t, sin_t, layer, n_layers, states, g):
    m, d = h.shape
    rq, rkv = q_norm.shape[0], kv_norm.shape[0]
    n = w_cat.shape[1]
    tm = _tile(math.gcd(g.mp, g.ts), 512)
    assert tm % g.s == 0
    nct = g.mp // tm
    const = lambda i: (0, 0)
    rows = lambda i: (i, 0)
    n_alias = 0 if states is None else 2
    colb = lambda i: (i, 0)
    return pl.pallas_call(
        functools.partial(_mla_down_body, rq=rq, rkv=rkv, seq=g.s, n_ctx_tiles=nct, n_alias=n_alias),
        grid=(m // tm,),
        in_specs=[pl.BlockSpec((tm, d), rows),
                  pl.BlockSpec((d, n), const, pipeline_mode=pl.Buffered(1)),
                  pl.BlockSpec((1, rq), const), pl.BlockSpec((1, rkv), const),
                  pl.BlockSpec((tm, LANE), rows), pl.BlockSpec((tm, LANE), rows)]
                 + [pl.BlockSpec(memory_space=pl.ANY)] * n_alias,
        out_specs=[pl.BlockSpec((tm, rq), rows), pl.BlockSpec((tm, rkv), rows), pl.BlockSpec((tm, LANE), rows),
                   _state_spec(tm, g.s, nct, layer, g.s, rkv, False, colb),
                   _state_spec(tm, g.s, nct, layer, g.s, MLA_ROPE, True, colb)],
        out_shape=[jax.ShapeDtypeStruct((m, rq), BF16), jax.ShapeDtypeStruct((m, rkv), F32),
                   jax.ShapeDtypeStruct((m, LANE), BF16),
                   jax.ShapeDtypeStruct((g.b, n_layers, g.s, rkv), F32),
                   jax.ShapeDtypeStruct((g.b, n_layers, MLA_ROPE, g.s), F32)],
        input_output_aliases={6 + a: 3 + a for a in range(n_alias)},
        compiler_params=_params(("arbitrary",), d * n * 2 + 2 * tm * (d * 2 + n * 8) + tm * n * 8 + 2**23),
        name="mla_down",
    )(h, w_cat, _row_vec(q_norm), _row_vec(kv_norm), cos_t, sin_t, *(states or ()))


def _store_col_tiles(o_ref, win, qt, value_t):
    for r in range((win.stop - win.start) // qt):
        o_ref[win.start // qt + r] = value_t[:, r * qt:(r + 1) * qt].astype(o_ref.dtype)


def _rope_t(x, cos, sin):
    q = ROT_DIM // 4
    rot = jnp.concatenate([-x[q:2 * q], x[:q], -x[3 * q:], x[2 * q:3 * q]], axis=0)
    return x * cos + rot * sin


def _mla_q_body(cq_ref, wt_ref, cos_ref, sin_ref, q_ref, *, heads, scale):
    qt = q_ref.shape[2]
    wins = _row_windows(cq_ref.shape[0], qt)
    accs = [_dot_nt(wt_ref[...], cq_ref[w, :]) for w in wins]
    for w, at in zip(wins, accs):
        cos, sin = cos_ref[:ROT_DIM, w], sin_ref[:ROT_DIM, w]
        for h in range(heads):
            lo = 2 * LANE * h
            pe = _rope_t(at[lo + LANE:lo + LANE + ROT_DIM], cos, sin)
            blk = jnp.concatenate([at[lo:lo + LANE], pe, at[lo + LANE + ROT_DIM:lo + 2 * LANE]], axis=0) * scale
            _store_col_tiles(q_ref.at[:, lo:lo + 2 * LANE, :], w, qt, blk)


def _mla_q_call(cq, w_uq_pad_t, cos_tt, sin_tt, heads, g):
    m, rq = cq.shape
    n = w_uq_pad_t.shape[0]
    tm = _tile(math.gcd(g.mp, g.ts), 512, max(g.qt, LANE))
    const = lambda i: (0, 0)
    scale = (MLA_NOPE + MLA_ROPE) ** -0.5 * LOG2E
    return pl.pallas_call(
        functools.partial(_mla_q_body, heads=heads, scale=scale),
        grid=(m // tm,),
        in_specs=[pl.BlockSpec((tm, rq), lambda i: (i, 0)),
                  pl.BlockSpec((n, rq), const, pipeline_mode=pl.Buffered(1)),
                  pl.BlockSpec((LANE, tm), lambda i: (0, i)), pl.BlockSpec((LANE, tm), lambda i: (0, i))],
        out_specs=pl.BlockSpec((tm // g.qt, n, g.qt), lambda i: (i, 0, 0)),
        out_shape=jax.ShapeDtypeStruct((m // g.qt, n, g.qt), BF16),
        compiler_params=_params(("arbitrary",), rq * n * 2 + 2 * tm * (rq * 2 + n * 2) + 4 * tm * n * 4 + 2**23),
        name="mla_q_up",
    )(cq, w_uq_pad_t, cos_tt, sin_tt)


def _kv_up_body(x_ref, kpe_ref, w_ref, kcat_ref, vt_ref, *, heads):
    wins = _row_windows(x_ref.shape[0], LANE)
    accs = [_dot(x_ref[w, :].astype(BF16), w_ref[...]) for w in wins]
    for w, a in zip(wins, accs):
        kpe = kpe_ref[w, :]
        for h in range(heads):
            lo = 2 * LANE * h
            kcat_ref[w, lo:lo + LANE] = a[:, lo:lo + LANE].astype(BF16)
            kcat_ref[w, lo + LANE:lo + 2 * LANE] = kpe
            vt_ref[h * LANE:(h + 1) * LANE, w] = a[:, lo + LANE:lo + 2 * LANE].T.astype(BF16)


def _kv_up_call(x, kpe16, w_all, j, heads, name):
    m, k = x.shape
    n = w_all.shape[2]
    tm = _tile(m, 512)
    rows = lambda i: (i, 0)
    return pl.pallas_call(
        functools.partial(_kv_up_body, heads=heads),
        grid=(m // tm,),
        in_specs=[pl.BlockSpec((tm, k), rows), pl.BlockSpec((tm, LANE), rows),
                  pl.BlockSpec((None, k, n), lambda i: (j, 0, 0), pipeline_mode=pl.Buffered(1))],
        out_specs=[pl.BlockSpec((tm, 2 * heads * LANE), rows), pl.BlockSpec((heads * LANE, tm), lambda i: (0, i))],
        out_shape=[jax.ShapeDtypeStruct((m, 2 * heads * LANE), BF16), jax.ShapeDtypeStruct((heads * LANE, m), BF16)],
        compiler_params=_params(("arbitrary",), k * n * 2 + 2 * tm * (k * 4 + n * 3) + tm * n * 4 + 2**23),
        name=name,
    )(x, kpe16, w_all)


def _row_fold(x, op):
    out = x[:SUBLANE]
    for r in range(1, x.shape[0] // SUBLANE):
        out = op(out, x[r * SUBLANE:(r + 1) * SUBLANE])
    return out


def _chunks(segments):
    out = []
    for k_ref, vt_ref in segments:
        n = k_ref.shape[0]
        assert n % LANE == 0
        out += [(k_ref, vt_ref, r0, min(KV_CHUNK, n - r0)) for r0 in range(0, n, KV_CHUNK)]
    return out


def _scores_pass(qt, segments, s_ref):
    m = None
    off = 0
    for k_ref, _, r0, n in _chunks(segments):
        s = _dot(k_ref[r0:r0 + n, :], qt)
        s_ref[off:off + n, :] = s
        cm = _row_fold(s, jnp.maximum)
        m = cm if m is None else jnp.maximum(m, cm)
        off += n
    return jnp.max(m, axis=0, keepdims=True)


def _values_pass(segments, s_ref, m):
    acc = l = None
    off = 0
    for _, vt_ref, r0, n in _chunks(segments):
        e = jnp.exp2(s_ref[off:off + n, :] - m)
        cl = _row_fold(e, jnp.add)
        pv = _dot(vt_ref[:, r0:r0 + n], e.astype(BF16))
        l = cl if l is None else l + cl
        acc = pv if acc is None else acc + pv
        off += n
    return acc * (1.0 / jnp.sum(l, axis=0, keepdims=True))


def _pipelined_tiles(n_tiles, scores_of, values_of, s_a, s_b):
    if n_tiles == 1:
        values_of(0, s_a, scores_of(0, s_a))
        return
    unroll = TILE_UNROLL if n_tiles % TILE_UNROLL == 0 else 2
    assert n_tiles % unroll == 0 and unroll % 2 == 0
    bufs = (s_a, s_b)

    def body(j, m):
        t = unroll * j
        for u in range(unroll):
            nxt = t + u + 1 if u + 1 < unroll else jnp.minimum(t + unroll, n_tiles - 1)
            m_nxt = scores_of(nxt, bufs[(u + 1) % 2])
            values_of(t + u, bufs[u % 2], m)
            m = m_nxt
        return m

    lax.fori_loop(0, n_tiles // unroll, body, scores_of(0, s_a))


def _tile_rows(t, n):
    return pl.ds(t * n, n) if isinstance(t, int) else pl.ds(pl.multiple_of(t * n, n), n)


def _pipelined_static(tiles, scores_of, values_of, bufs):
    grp = len(bufs) // 2
    groups = [tiles[i:i + grp] for i in range(0, len(tiles), grp)]
    prev = None
    for gi, group in enumerate(groups):
        half = bufs[(gi % 2) * grp:(gi % 2 + 1) * grp]
        cur = [(tile, buf, scores_of(tile, buf)) for tile, buf in zip(group, half)]
        for tile, buf, m in prev or ():
            values_of(tile, buf, m)
        prev = cur
    for tile, buf, m in prev:
        values_of(tile, buf, m)


def _mla_attn_ctx_body(q_ref, k_ref, vt_ref, o_ref, *bufs, heads):
    nq = q_ref.shape[2]
    seg = lambda h: [(k_ref.at[:, 2 * LANE * h:2 * LANE * (h + 1)], vt_ref.at[LANE * h:LANE * (h + 1), :])]

    def scores_of(tile, s_ref):
        h, r = tile
        return _scores_pass(q_ref[r, 2 * LANE * h:2 * LANE * (h + 1), :], seg(h), s_ref)

    def values_of(tile, s_ref, m):
        h, r = tile
        o_ref[r * nq:(r + 1) * nq, LANE * h:LANE * (h + 1)] = _values_pass(seg(h), s_ref, m).T.astype(BF16)

    tiles = [(h, r) for h in range(heads) for r in range(q_ref.shape[0])]
    _pipelined_static(tiles, scores_of, values_of, bufs)


def _mla_attn_lat_body(q_ref, kc_ref, vc_ref, kn_ref, vn_ref, o_alias, o_ref, s_a, s_b):
    del o_alias
    segs = [(kc_ref, vc_ref), (kn_ref, vn_ref)]
    nq = q_ref.shape[2]

    def scores_of(t, s_ref):
        return _scores_pass(q_ref[t], segs, s_ref)

    def values_of(t, s_ref, m):
        o_ref[_tile_rows(t, nq), :] = _values_pass(segs, s_ref, m).T.astype(BF16)

    _pipelined_tiles(q_ref.shape[0], scores_of, values_of, s_a, s_b)


def _mla_attn(q, kcat, vt, kcat_c, vt_c, heads, g):
    m = kcat.shape[0]
    hn = heads * LANE
    past = kcat_c.shape[0] // g.bs
    o = pl.pallas_call(
        functools.partial(_mla_attn_ctx_body, heads=heads),
        grid=(g.b,),
        in_specs=[pl.BlockSpec((g.s // g.qt, 2 * hn, g.qt), lambda b: (b, 0, 0)),
                  pl.BlockSpec((g.s, 2 * hn), lambda b: (b, 0)),
                  pl.BlockSpec((hn, g.s), lambda b: (0, b))],
        out_specs=pl.BlockSpec((g.s, hn), lambda b: (b, 0)),
        out_shape=jax.ShapeDtypeStruct((m, hn), BF16),
        scratch_shapes=[pltpu.VMEM((g.s, g.qt), F32)] * (2 * CTX_GROUP),
        compiler_params=_params(("arbitrary",), 2 * g.s * hn * 12 + 2**24),
        name="mla_attn_ctx",
    )(q, kcat, vt)
    kb0 = g.mp // g.ts
    return pl.pallas_call(
        _mla_attn_lat_body,
        grid=(g.bs, heads),
        in_specs=[pl.BlockSpec((g.ts // g.qt, 2 * LANE, g.qt), lambda b, h: (kb0 + b, h, 0)),
                  pl.BlockSpec((past, 2 * LANE), lambda b, h: (b, h)),
                  pl.BlockSpec((LANE, past), lambda b, h: (h, b)),
                  pl.BlockSpec((g.ts, 2 * LANE), lambda b, h: (kb0 + b, h)),
                  pl.BlockSpec((LANE, g.ts), lambda b, h: (h, kb0 + b)),
                  pl.BlockSpec(memory_space=pl.ANY)],
        out_specs=pl.BlockSpec((g.ts, LANE), lambda b, h: (kb0 + b, h)),
        out_shape=jax.ShapeDtypeStruct((m, hn), BF16),
        input_output_aliases={5: 0},
        scratch_shapes=[pltpu.VMEM((past + g.ts, g.qt), F32)] * 2,
        compiler_params=_params(("arbitrary", "arbitrary"), 2**25),
        name="mla_attn_lat",
    )(q, kcat_c, vt_c, kcat, vt, o)


def _diff_lambda(lam_ref, lam_init):
    lv = lam_ref[...]
    s1 = jnp.sum(lv[0:1] * lv[1:2], axis=-1, keepdims=True)
    s2 = jnp.sum(lv[2:3] * lv[3:4], axis=-1, keepdims=True)
    return jnp.exp(s1) - jnp.exp(s2) + lam_init


def _diff_queries_t(qt):
    qf = qt.astype(F32)
    row = lax.broadcasted_iota(jnp.int32, qf.shape, 0)
    return jnp.concatenate([jnp.where(row < DIFF_DK, qf, 0.0), jnp.where(row < DIFF_DK, 0.0, qf)],
                           axis=1).astype(BF16)


def _diff_combine(ot, lam, subln_col, lam_init):
    n = ot.shape[1] // 2
    dt = ot[:, :n] - lam * ot[:, n:]
    dt = dt * lax.rsqrt(jnp.mean(dt * dt, axis=0, keepdims=True) + NORM_EPS) * subln_col * (1.0 - lam_init)
    return dt.T.astype(BF16)


def _diff_attn_ctx_body(q_ref, k_ref, vt_ref, lam_ref, sub_ref, o_ref, *bufs, heads, lam_init):
    lam = _diff_lambda(lam_ref, lam_init)
    nq = q_ref.shape[2]
    cols = lambda h: slice(LANE * h, LANE * (h + 1))
    seg = lambda h: [(k_ref.at[:, cols(h)], vt_ref.at[cols(h), :])]

    def scores_of(tile, s_ref):
        h, r = tile
        return _scores_pass(_diff_queries_t(q_ref[r, cols(h), :]), seg(h), s_ref)

    def values_of(tile, s_ref, m):
        h, r = tile
        o_ref[r * nq:(r + 1) * nq, cols(h)] = _diff_combine(_values_pass(seg(h), s_ref, m), lam, sub_ref[...],
                                                           lam_init)

    tiles = [(h, r) for h in range(heads) for r in range(q_ref.shape[0])]
    _pipelined_static(tiles, scores_of, values_of, bufs)


def _diff_attn_lat_body(q_ref, kc_ref, vc_ref, kn_ref, vn_ref, lam_ref, sub_ref, o_alias, o_ref, s_a, s_b, *,
                        lam_init):
    del o_alias
    lam = _diff_lambda(lam_ref, lam_init)
    segs = [(kc_ref, vc_ref), (kn_ref, vn_ref)]
    nq = q_ref.shape[2]

    def scores_of(t, s_ref):
        return _scores_pass(_diff_queries_t(q_ref[t]), segs, s_ref)

    def values_of(t, s_ref, m):
        o_ref[_tile_rows(t, nq), :] = _diff_combine(_values_pass(segs, s_ref, m), lam, sub_ref[...], lam_init)

    _pipelined_tiles(q_ref.shape[0], scores_of, values_of, s_a, s_b)


def _diff_attn(q, k, vt, k_c, vt_c, lam_vecs, subln, lam_init, heads, g):
    m = k.shape[0]
    hn = heads * LANE
    past = k_c.shape[0] // g.bs
    sub = subln.reshape(-1, 1)
    blk = lambda b: (b, 0)
    qd = g.qt // 2
    o = pl.pallas_call(
        functools.partial(_diff_attn_ctx_body, heads=heads, lam_init=lam_init),
        grid=(g.b,),
        in_specs=[pl.BlockSpec((g.s // qd, hn, qd), lambda b: (b, 0, 0)), pl.BlockSpec((g.s, hn), blk),
                  pl.BlockSpec((hn, g.s), lambda b: (0, b)),
                  pl.BlockSpec((4, DIFF_DK), lambda b: (0, 0)), pl.BlockSpec((DIFF_DV, 1), lambda b: (0, 0))],
        out_specs=pl.BlockSpec((g.s, hn), blk),
        out_shape=jax.ShapeDtypeStruct((m, hn), BF16),
        scratch_shapes=[pltpu.VMEM((g.s, g.qt), F32)] * (2 * CTX_GROUP),
        compiler_params=_params(("arbitrary",), 2 * g.s * hn * 12 + 2**24),
        name="diff_attn_ctx",
    )(q, k, vt, lam_vecs, sub)
    kb0 = g.mp // g.ts
    return pl.pallas_call(
        functools.partial(_diff_attn_lat_body, lam_init=lam_init),
        grid=(g.bs, heads),
        in_specs=[pl.BlockSpec((g.ts // qd, LANE, qd), lambda b, h: (kb0 + b, h, 0)),
                  pl.BlockSpec((past, LANE), lambda b, h: (b, h)),
                  pl.BlockSpec((LANE, past), lambda b, h: (h, b)),
                  pl.BlockSpec((g.ts, LANE), lambda b, h: (kb0 + b, h)),
                  pl.BlockSpec((LANE, g.ts), lambda b, h: (h, kb0 + b)),
                  pl.BlockSpec((4, DIFF_DK), lambda b, h: (0, 0)),
                  pl.BlockSpec((DIFF_DV, 1), lambda b, h: (0, 0)),
                  pl.BlockSpec(memory_space=pl.ANY)],
        out_specs=pl.BlockSpec((g.ts, LANE), lambda b, h: (kb0 + b, h)),
        out_shape=jax.ShapeDtypeStruct((m, hn), BF16),
        input_output_aliases={7: 0},
        scratch_shapes=[pltpu.VMEM((past + g.ts, g.qt), F32)] * 2,
        compiler_params=_params(("arbitrary", "arbitrary"), 2**25),
        name="diff_attn_lat",
    )(q, k_c, vt_c, k, vt, lam_vecs, sub, o)


def _proj_body(h_ref, wt_ref, cos_ref, sin_ref, *refs, rope, scale, kinds, seq, n_ctx_tiles, n_alias):
    out_refs = refs[n_alias:]
    i = pl.program_id(1)
    wins = _row_windows(h_ref.shape[0], max(seq, LANE))
    accs = [_dot_nt(wt_ref[...], h_ref[w, :]) for w in wins]
    for w, at in zip(wins, accs):
        if rope:
            cos, sin = cos_ref[:ROT_DIM, w], sin_ref[:ROT_DIM, w]
            at = jnp.concatenate([_rope_t(at[r:r + ROT_DIM], cos, sin) for r in range(0, at.shape[0], ROT_DIM)],
                                 axis=0)
        if scale != 1.0:
            at = at * scale
        for o_ref, kind in zip(out_refs, kinds):
            if kind == "rows":
                o_ref[w, :] = at.T.astype(o_ref.dtype)
            elif kind == "cols":
                o_ref[:, w] = at.astype(o_ref.dtype)
            elif kind == "col_tiles":
                _store_col_tiles(o_ref, w, o_ref.shape[2], at)
            else:
                @pl.when(i < n_ctx_tiles)
                def _(o_ref=o_ref, kind=kind, at=at, w=w):
                    _store_state(o_ref, w, seq, at if kind == "state_cols" else at.T, kind == "state_cols")


def _proj_call(h, wt_all, layer, col0, ncols, cos_tt, sin_tt, rope, scale, outs, st_layer, st_layers, state, g, name):
    m, d = h.shape
    tn = _tile(ncols, 1024, LANE)
    tm = _tile(math.gcd(g.mp, g.ts), 512, LANE)
    assert tm % g.s == 0
    nct = g.mp // tm
    nb0 = col0 // tn
    colb = lambda j, i: (i, j)
    specs, shapes = [], []
    for kind, dt in outs:
        if kind == "rows":
            specs.append(pl.BlockSpec((tm, tn), lambda j, i: (i, j)))
            shapes.append(jax.ShapeDtypeStruct((m, ncols), dt))
        elif kind == "cols":
            specs.append(pl.BlockSpec((tn, tm), lambda j, i: (j, i)))
            shapes.append(jax.ShapeDtypeStruct((ncols, m), dt))
        elif kind == "col_tiles":
            qd = g.qt // 2
            specs.append(pl.BlockSpec((tm // qd, tn, qd), lambda j, i: (i, j, 0)))
            shapes.append(jax.ShapeDtypeStruct((m // qd, ncols, qd), dt))
        else:
            tr = kind == "state_cols"
            specs.append(_state_spec(tm, g.s, nct, st_layer, g.s, tn, tr, colb))
            shapes.append(jax.ShapeDtypeStruct((g.b, st_layers, ncols, g.s) if tr else (g.b, st_layers, g.s, ncols), dt))
    st_idx = [k for k, (kind, _) in enumerate(outs) if kind.startswith("state")]
    n_alias = len(st_idx) if state is not None else 0
    return pl.pallas_call(
        functools.partial(_proj_body, rope=rope, scale=scale, kinds=tuple(k for k, _ in outs), seq=g.s,
                          n_ctx_tiles=nct, n_alias=n_alias),
        grid=(ncols // tn, m // tm),
        in_specs=[pl.BlockSpec((tm, d), lambda j, i: (i, 0)),
                  pl.BlockSpec((None, tn, d), lambda j, i: (layer, nb0 + j, 0)),
                  pl.BlockSpec((LANE, tm), lambda j, i: (0, i)), pl.BlockSpec((LANE, tm), lambda j, i: (0, i))]
                 + [pl.BlockSpec(memory_space=pl.ANY)] * n_alias,
        out_specs=specs,
        out_shape=shapes,
        input_output_aliases={4 + a: st_idx[a] for a in range(n_alias)},
        compiler_params=_params(("arbitrary", "arbitrary"),
                                d * tn * 4 + 2 * tm * (d * 2 + tn * 6) + 4 * tm * tn * 4 + 2**23),
        name=name,
    )(h, wt_all, cos_tt, sin_tt, *([state] if n_alias else []))


def _out_ln_body(o_ref, w_ref, x_ref, gate_ref, g_ref, b_ref, sc_ref, sh_ref, xo_ref, ho_ref, *, alpha):
    wins = _row_windows(o_ref.shape[0], 16)
    accs = [_dot(o_ref[w, :], w_ref[...]) for w in wins]
    for w, y in zip(wins, accs):
        xn = _layer_norm(alpha * x_ref[w, :] + gate_ref[...] * y, g_ref[...], b_ref[...])
        xo_ref[w, :] = xn
        ho_ref[w, :] = (xn * (1.0 + sc_ref[...]) + sh_ref[...]).astype(BF16)


def _out_ln_call(o, w_o_all, layer, x, mod, ln_g, ln_b, alpha, g):
    m, k = o.shape
    d = w_o_all.shape[2]
    tm = _tile(math.gcd(g.mp, g.ts), 512)
    row = lambda i: _mod_row(i, tm, g.mp, g.ts)
    const = lambda i: (0, 0)
    rows = lambda i: (i, 0)
    return pl.pallas_call(
        functools.partial(_out_ln_body, alpha=alpha),
        grid=(m // tm,),
        in_specs=[pl.BlockSpec((tm, k), rows),
                  pl.BlockSpec((None, k, d), lambda i: (layer, 0, 0), pipeline_mode=pl.Buffered(1)),
                  pl.BlockSpec((tm, d), rows), _mod_spec(2, d, row),
                  pl.BlockSpec((1, d), const), pl.BlockSpec((1, d), const),
                  _mod_spec(4, d, row), _mod_spec(3, d, row)],
        out_specs=[pl.BlockSpec((tm, d), rows), pl.BlockSpec((tm, d), rows)],
        out_shape=[jax.ShapeDtypeStruct((m, d), F32), jax.ShapeDtypeStruct((m, d), BF16)],
        compiler_params=_params(("arbitrary",), k * d * 2 + 2 * tm * (k * 2 + d * 10) + 4 * tm * d * 4 + 2**23),
        name="out_proj_ln",
    )(o, w_o_all, x, mod, _row_vec(ln_g), _row_vec(ln_b), mod, mod)


def _lane_cumsum(x, tri):
    c = tri.shape[0]
    carry = jnp.zeros((x.shape[0], 1), F32)
    parts = []
    for j in range(x.shape[1] // c):
        xc = x[:, j * c:(j + 1) * c]
        parts.append(_dot(xc.astype(BF16), tri) + carry)
        carry = carry + jnp.sum(xc, axis=1, keepdims=True)
    return parts[0] if len(parts) == 1 else jnp.concatenate(parts, axis=1)


def _route_body(h_ref, wr_ref, slot_ref, gate_ref, *, groups, t, cap):
    logits = _dot_nt(wr_ref[...].astype(BF16), h_ref[...])
    ex = jnp.exp(logits - jnp.max(logits, axis=0, keepdims=True))
    aff = ex / jnp.sum(ex, axis=0, keepdims=True)
    nbits = jnp.finfo(aff.dtype).bits
    keys = lax.bitcast_convert_type(aff, jnp.dtype(f"int{nbits}"))
    c = min(t, 2 * LANE)
    tri = jnp.where(lax.broadcasted_iota(jnp.int32, (c, c), 0) <= lax.broadcasted_iota(jnp.int32, (c, c), 1),
                    1.0, 0.0).astype(BF16)
    capf = float(cap)
    for gi in range(groups):
        k = keys[:, gi * t:(gi + 1) * t]
        thr = jnp.zeros((k.shape[0], 1), keys.dtype)
        for bit in range(nbits - 2, -1, -1):
            cand = thr | (1 << bit)
            cnt = jnp.sum(jnp.where(k >= cand, 1.0, 0.0), axis=1, keepdims=True)
            thr = jnp.where(cnt >= capf, cand, thr)
        gt = jnp.where(k > thr, 1.0, 0.0)
        eq = jnp.where(k == thr, 1.0, 0.0)
        need = capf - jnp.sum(gt, axis=1, keepdims=True)
        eq_before = _lane_cumsum(eq, tri) - eq
        sel = gt + eq * jnp.where(eq_before < need, 1.0, 0.0)
        pos = _lane_cumsum(sel, tri) - 1.0
        slot_ref[gi] = jnp.where(sel > 0.5, pos, -1.0).astype(jnp.int32)
        gate_ref[gi] = aff[:, gi * t:(gi + 1) * t]


def _route_call(h2, w_router_t, row0, nreq, t, cap, groups, name):
    e, d = w_router_t.shape
    rb0 = row0 // (groups * t)
    return pl.pallas_call(
        functools.partial(_route_body, groups=groups, t=t, cap=cap),
        grid=(nreq // groups,),
        in_specs=[pl.BlockSpec((groups * t, d), lambda i: (rb0 + i, 0)), pl.BlockSpec((e, d), lambda i: (0, 0))],
        out_specs=[pl.BlockSpec((groups, e, t), lambda i: (i, 0, 0)), pl.BlockSpec((groups, e, t), lambda i: (i, 0, 0))],
        out_shape=[jax.ShapeDtypeStruct((nreq, e, t), jnp.int32), jax.ShapeDtypeStruct((nreq, e, t), F32)],
        compiler_params=_params(("parallel",), 2 * groups * t * d * 2 + 2**24),
        name=name,
    )(h2, w_router_t)


def _one_hot_rows(slot_row, cap):
    r = lax.broadcasted_iota(jnp.int32, (cap, slot_row.shape[1]), 0)
    return slot_row == r


def _gather_ctx_body(h_ref, slot_ref, gate_ref, xs_ref, gs_ref, *, cap):
    nreq, ne, s = slot_ref.shape
    for r in range(nreq):
        masks = [_one_hot_rows(slot_ref[r, e:e + 1, :], cap) for e in range(ne)]
        p = jnp.concatenate([jnp.where(mk, 1.0, 0.0) for mk in masks], axis=0).astype(BF16)
        xs = _dot(p, h_ref[r * s:(r + 1) * s, :]).astype(BF16)
        for e in range(ne):
            xs_ref[e, r * cap:(r + 1) * cap] = xs[e * cap:(e + 1) * cap]
            gs_ref[e, r * cap:(r + 1) * cap] = jnp.sum(jnp.where(masks[e], gate_ref[r, e:e + 1, :], 0.0),
                                                       axis=1, keepdims=True)


def _gather_lat_body(h_ref, slot_ref, gate_ref, xs_alias, gs_alias, xs_ref, gs_ref, *, cap):
    del xs_alias, gs_alias
    e = pl.program_id(1)
    mk = _one_hot_rows(slot_ref[pl.ds(e, 1), :], cap)
    xs_ref[...] = _dot(jnp.where(mk, 1.0, 0.0).astype(BF16), h_ref[...]).astype(BF16)
    gs_ref[...] = jnp.sum(jnp.where(mk, gate_ref[pl.ds(e, 1), :], 0.0), axis=1, keepdims=True)


def _gather(h2, slot_p, gate_p, slot_s, gate_s, g):
    d = h2.shape[1]
    ne = slot_p.shape[1]
    rows_e = g.b * g.cap_p + g.bs * g.cap_s
    shapes = [jax.ShapeDtypeStruct((ne, rows_e, d), BF16), jax.ShapeDtypeStruct((ne, rows_e, 1), F32)]
    nr = g.ctx_reqs
    xs, gs = pl.pallas_call(
        functools.partial(_gather_ctx_body, cap=g.cap_p),
        grid=(g.b // nr,),
        in_specs=[pl.BlockSpec((nr * g.s, d), lambda b: (b, 0)),
                  pl.BlockSpec((nr, ne, g.s), lambda b: (b, 0, 0)), pl.BlockSpec((nr, ne, g.s), lambda b: (b, 0, 0))],
        out_specs=[pl.BlockSpec((ne, nr * g.cap_p, d), lambda b: (0, b, 0)),
                   pl.BlockSpec((ne, nr * g.cap_p, 1), lambda b: (0, b, 0))],
        out_shape=shapes,
        compiler_params=_params(("parallel",), 2 * nr * ne * g.cap_p * (d * 8 + g.s * 8) + 2**24),
        name="moe_gather_ctx",
    )(h2, slot_p, gate_p)
    blk0 = (g.b * g.cap_p) // g.cap_s
    hb0 = g.mp // g.ts
    return pl.pallas_call(
        functools.partial(_gather_lat_body, cap=g.cap_s),
        grid=(g.bs, ne),
        in_specs=[pl.BlockSpec((g.ts, d), lambda b, e: (hb0 + b, 0)),
                  pl.BlockSpec((None, ne, g.ts), lambda b, e: (b, 0, 0)), pl.BlockSpec((None, ne, g.ts), lambda b, e: (b, 0, 0)),
                  pl.BlockSpec(memory_space=pl.ANY), pl.BlockSpec(memory_space=pl.ANY)],
        out_specs=[pl.BlockSpec((None, g.cap_s, d), lambda b, e: (e, blk0 + b, 0)),
                   pl.BlockSpec((None, g.cap_s, 1), lambda b, e: (e, blk0 + b, 0))],
        out_shape=shapes,
        input_output_aliases={3: 0, 4: 1},
        compiler_params=_params(("parallel", "parallel"), 2 * g.ts * d * 2 + g.cap_s * (g.ts * 12 + d * 12) + 2**24),
        name="moe_gather_lat",
    )(h2, slot_s, gate_s, xs, gs)


def _ffn1_body(x_ref, wg_ref, wu_ref, o_ref):
    x = x_ref[...]
    a = _dot(x, wg_ref[...].astype(BF16))
    u = _dot(x, wu_ref[...].astype(BF16))
    o_ref[...] = (a * _sigmoid(a) * u).astype(BF16)


def _ffn1_call(xs, w_gate, w_up, layer):
    ne, rows, d = xs.shape
    f = w_gate.shape[3]
    tn = _tile(f, 512, LANE)
    wspec = pl.BlockSpec((None, None, d, tn), lambda e, n: (layer, e, 0, n))
    return pl.pallas_call(
        _ffn1_body,
        grid=(ne, f // tn),
        in_specs=[pl.BlockSpec((None, rows, d), lambda e, n: (e, 0, 0)), wspec, wspec],
        out_specs=pl.BlockSpec((None, rows, tn), lambda e, n: (e, 0, n)),
        out_shape=jax.ShapeDtypeStruct((ne, rows, f), BF16),
        compiler_params=_params(("parallel", "parallel"), 2 * rows * d * 2 + 4 * d * tn * 4 + 2 * d * tn * 2 + 5 * rows * tn * 4 + 2**23),
        name="moe_ffn_gate_up",
    )(xs, w_gate, w_up)


def _ffn2_body(h_ref, w_ref, gs_ref, o_ref):
    o_ref[...] = (_dot(h_ref[...], w_ref[...].astype(BF16)) * gs_ref[...]).astype(BF16)


def _ffn2_call(hm, w_down, gs, layer):
    ne, rows, f = hm.shape
    d = w_down.shape[3]
    tn = _tile(d, 512, LANE)
    return pl.pallas_call(
        _ffn2_body,
        grid=(ne, d // tn),
        in_specs=[pl.BlockSpec((None, rows, f), lambda e, n: (e, 0, 0)),
                  pl.BlockSpec((None, None, f, tn), lambda e, n: (layer, e, 0, n)),
                  pl.BlockSpec((None, rows, 1), lambda e, n: (e, 0, 0))],
        out_specs=pl.BlockSpec((None, rows, tn), lambda e, n: (e, 0, n)),
        out_shape=jax.ShapeDtypeStruct((ne, rows, d), BF16),
        compiler_params=_params(("parallel", "parallel"), 2 * rows * f * 2 + 2 * f * tn * 4 + f * tn * 2 + 3 * rows * tn * 4 + 2 * rows * LANE * 4 + 2**23),
        name="moe_ffn_down",
    )(hm, w_down, gs)


def _combine(y, slot, cap):
    ne = y.shape[0]
    p = jnp.concatenate([jnp.where(_one_hot_rows(slot[e:e + 1, :], cap), 1.0, 0.0) for e in range(ne)], axis=0)
    return _dot_tn(p.astype(BF16), y.reshape(ne * cap, y.shape[2]))


def _combine_ln_body(*refs, cap, alpha, n_alias, emit_h, nreq):
    y_ref, slot_ref, x_ref, gate_ref, g_ref, b_ref = refs[:6]
    rest = refs[6:]
    if emit_h:
        sc_ref, sh_ref = rest[:2]
        rest = rest[2:]
    outs = rest[n_alias:]
    s = x_ref.shape[0] // max(nreq, 1)
    for r in range(max(nreq, 1)):
        rows = slice(r * s, (r + 1) * s)
        slot = slot_ref[r] if nreq else slot_ref[...]
        moe = _combine(y_ref[:, r * cap:(r + 1) * cap, :], slot, cap)
        xn = _layer_norm(alpha * x_ref[rows, :] + gate_ref[...] * moe, g_ref[...], b_ref[...])
        outs[0][rows, :] = xn
        if emit_h:
            outs[1][rows, :] = (xn * (1.0 + sc_ref[...]) + sh_ref[...]).astype(BF16)


def _combine_ln(y, slot_p, slot_s, x, mod, mod_next, ln_g, ln_b, alpha, g):
    ne, rows_e, d = y.shape
    m = x.shape[0]
    emit_h = mod_next is not None
    lg, lb = _row_vec(ln_g), _row_vec(ln_b)
    shapes = [jax.ShapeDtypeStruct((m, d), F32), jax.ShapeDtypeStruct((m, d), BF16)] if emit_h else \
        [jax.ShapeDtypeStruct((g.mp, d), F32)]
    n_out = len(shapes)

    def specs(row):
        sp = [pl.BlockSpec((1, d), lambda *a: (0, 0)), pl.BlockSpec((1, d), lambda *a: (0, 0))]
        md = [_mod_spec(5, d, row)]
        nx = [_mod_spec(1, d, row), _mod_spec(0, d, row)] if emit_h else []
        return md, sp, nx

    nr = g.ctx_reqs
    md, sp, nx = specs(lambda b: 0)
    outs = pl.pallas_call(
        functools.partial(_combine_ln_body, cap=g.cap_p, alpha=alpha, n_alias=0, emit_h=emit_h, nreq=nr),
        grid=(g.b // nr,),
        in_specs=[pl.BlockSpec((ne, nr * g.cap_p, d), lambda b: (0, b, 0)),
                  pl.BlockSpec((nr, ne, g.s), lambda b: (b, 0, 0)),
                  pl.BlockSpec((nr * g.s, d), lambda b: (b, 0))] + md + sp + nx,
        out_specs=[pl.BlockSpec((nr * g.s, d), lambda b: (b, 0))] * n_out,
        out_shape=shapes,
        compiler_params=_params(("parallel",), 2 * nr * (ne * g.cap_p * (d * 2 + g.s * 8) + g.s * d * 40) + 2**24),
        name="moe_combine_ln_ctx",
    )(y, slot_p, x, mod, lg, lb, *([mod_next, mod_next] if emit_h else []))
    outs = list(outs) if isinstance(outs, (list, tuple)) else [outs]
    tt = _tile(g.ts, 256, LANE)
    nt = g.ts // tt
    blk0 = (g.b * g.cap_p) // g.cap_s
    xb0 = g.mp // tt
    md, sp, nx = specs(lambda b, i: 1 + b)
    n_in = 6 + len(nx)
    n_alias = n_out if emit_h else 0
    ob0 = xb0 if emit_h else 0
    outs_s = pl.pallas_call(
        functools.partial(_combine_ln_body, cap=g.cap_s, alpha=alpha, n_alias=n_alias, emit_h=emit_h, nreq=0),
        grid=(g.bs, nt),
        in_specs=[pl.BlockSpec((ne, g.cap_s, d), lambda b, i: (0, blk0 + b, 0), pipeline_mode=pl.Buffered(1)),
                  pl.BlockSpec((None, ne, tt), lambda b, i: (b, 0, i)),
                  pl.BlockSpec((tt, d), lambda b, i: (xb0 + b * nt + i, 0))] + md + sp + nx
                 + [pl.BlockSpec(memory_space=pl.ANY)] * n_alias,
        out_specs=[pl.BlockSpec((tt, d), lambda b, i: (ob0 + b * nt + i, 0))] * n_out,
        out_shape=shapes if emit_h else [jax.ShapeDtypeStruct((g.ms, d), F32)],
        input_output_aliases={n_in + j: j for j in range(n_alias)},
        compiler_params=_params(("parallel", "parallel"), ne * g.cap_s * (d * 2 + tt * 10) + tt * d * 48 + 2**24),
        name="moe_combine_ln_lat",
    )(y, slot_s, x, mod, lg, lb, *([mod_next, mod_next] if emit_h else []), *(outs if emit_h else []))
    outs_s = list(outs_s) if isinstance(outs_s, (list, tuple)) else [outs_s]
    if emit_h:
        return outs_s[0], outs_s[1]
    return outs[0], outs_s[0]


class _Geom:
    def __init__(self, b, s, bs, ts, ne):
        self.b, self.s, self.bs, self.ts = b, s, bs, ts
        self.mp, self.ms = b * s, bs * ts
        self.cap_p = CAP_FACTOR * s // ne
        self.cap_s = CAP_FACTOR * ts // ne
        self.qt = min(Q_TILE, s)
        self.ctx_reqs = 2 if b % 2 == 0 else 1
        assert ts % s == 0 and self.mp % ts == 0 and bs + 1 <= MOD_ROWS
        assert (b * self.cap_p) % self.cap_s == 0 and self.cap_p % 16 == 0


def _rope_tables(g):
    rows = g.ts // GRID_W
    row = jnp.repeat(jnp.arange(rows, dtype=F32), GRID_W)
    col = jnp.tile(jnp.arange(GRID_W, dtype=F32), rows)
    half = ROT_DIM // 2
    inv_freq = ROPE_BASE ** (-jnp.arange(0, half, 2, dtype=F32) / half)
    ang_r = row[:, None] * inv_freq
    ang_c = col[:, None] * inv_freq
    ang = jnp.concatenate([ang_r, ang_r, ang_c, ang_c] * (LANE // ROT_DIM), axis=-1)
    cos = jnp.concatenate([jnp.ones((g.mp, LANE), F32), jnp.tile(jnp.cos(ang), (g.bs, 1))], axis=0)
    sin = jnp.concatenate([jnp.zeros((g.mp, LANE), F32), jnp.tile(jnp.sin(ang), (g.bs, 1))], axis=0)
    return cos, sin


def _pad_heads(w, heads, width):
    k = w.shape[0]
    per = w.shape[1] // heads
    return jnp.pad(w.reshape(k, heads, per), ((0, 0), (0, 0), (0, width - per))).reshape(k, heads * width)


def kernel(x_prompt, x_sample, cache_mla_ckv, cache_mla_kpe, cache_diff_k, cache_diff_v, c, c_ctx, ada_w, ada_b, ln1_g, ln1_b, ln2_g, ln2_b, mla_w_dq, mla_q_norm, mla_w_uq, mla_w_dkv, mla_kv_norm, mla_w_ukv, mla_w_o, diff_w_qkv, diff_lambda_q1, diff_lambda_k1, diff_lambda_q2, diff_lambda_k2, diff_subln, diff_w_o, moe_w_router, moe_w_gate, moe_w_up, moe_w_down):
    b, s, d = x_prompt.shape
    bs, ts, _ = x_sample.shape
    depth = ada_w.shape[0]
    ne = moe_w_router.shape[-1]
    g = _Geom(b, s, bs, ts, ne)
    mla_heads = mla_w_o.shape[1] // MLA_V
    diff_heads = diff_w_o.shape[1] // DIFF_DV
    alpha = (2.0 * depth) ** 0.25
    past = cache_mla_ckv.shape[2]

    cv = jnp.concatenate([c_ctx[None, :], c, jnp.zeros((MOD_ROWS - 1 - bs, d), F32)], axis=0)
    mods = _mod_call(cv, ada_w, ada_b)
    mod_of = lambda l: mods[l].reshape(MOD_ROWS, 1, 6 * d)
    cos_t, sin_t = _rope_tables(g)
    x, h = _modcast_call(x_prompt.reshape(g.mp, d), x_sample.reshape(g.ms, d), mod_of(0), g)

    n_mla, n_diff = mla_w_o.shape[0], diff_w_o.shape[0]
    mla_w_ukv, mla_w_o, diff_w_o = (w.astype(BF16) for w in (mla_w_ukv, mla_w_o, diff_w_o))
    diff_wt_qkv = jnp.swapaxes(diff_w_qkv, 1, 2).astype(BF16)
    cos_tt, sin_tt = cos_t.T, sin_t.T
    st_mla = st_dk = st_dv = None
    for l in range(depth):
        mod = mod_of(l)
        j = l // 2
        if l % 2 == 0:
            w_down = jnp.pad(jnp.concatenate([mla_w_dq[j], mla_w_dkv[j]], axis=1),
                             ((0, 0), (0, LANE - MLA_ROPE))).astype(BF16)
            cq, ckv, kpe16, *st_mla = _mla_down_call(h, w_down, mla_q_norm[j], mla_kv_norm[j], cos_t, sin_t,
                                                     j, n_mla, st_mla, g)
            w_uq_t = _pad_heads(mla_w_uq[j], mla_heads, 2 * LANE).T.astype(BF16)
            q = _mla_q_call(cq, w_uq_t, cos_tt, sin_tt, mla_heads, g)
            kpe_c = jnp.pad(cache_mla_kpe[:, j], ((0, 0), (0, 0), (0, LANE - MLA_ROPE))).astype(BF16)
            kcat, vt = _kv_up_call(ckv, kpe16, mla_w_ukv, j, mla_heads, "mla_kv_up")
            kcat_c, vt_c = _kv_up_call(cache_mla_ckv[:, j].reshape(bs * past, -1), kpe_c.reshape(bs * past, LANE),
                                       mla_w_ukv, j, mla_heads, "mla_kv_up_cache")
            o = _mla_attn(q, kcat, vt, kcat_c, vt_c, mla_heads, g)
            w_o = mla_w_o
        else:
            lam_init = 0.8 - 0.6 * math.exp(-0.3 * l)
            nq = diff_heads * 2 * DIFF_DK
            nv = diff_heads * DIFF_DV
            q, = _proj_call(h, diff_wt_qkv, j, 0, nq, cos_tt, sin_tt, True, DIFF_DK ** -0.5 * LOG2E,
                            [("col_tiles", BF16)], j, n_diff, None, g, "diff_q")
            k16, st_dk = _proj_call(h, diff_wt_qkv, j, nq, nq, cos_tt, sin_tt, True, 1.0,
                                    [("rows", BF16), ("state_cols", F32)], j, n_diff, st_dk, g, "diff_k")
            vt16, st_dv = _proj_call(h, diff_wt_qkv, j, 2 * nq, nv, cos_tt, sin_tt, False, 1.0,
                                     [("cols", BF16), ("state_rows", F32)], j, n_diff, st_dv, g, "diff_v")
            k_c = cache_diff_k[:, j].reshape(bs * past, nq).astype(BF16)
            vt_c = cache_diff_v[:, j].reshape(bs * past, nv).T.astype(BF16)
            lam_vecs = jnp.stack([diff_lambda_q1[j], diff_lambda_k1[j], diff_lambda_q2[j], diff_lambda_k2[j]])
            o = _diff_attn(q, k16, vt16, k_c, vt_c, lam_vecs, diff_subln[j], lam_init, diff_heads, g)
            w_o = diff_w_o
        x, h2 = _out_ln_call(o, w_o, j, x, mod, ln1_g[l], ln1_b[l], alpha, g)

        w_rt = moe_w_router[l].T
        groups = max(1, min(g.b, g.ts // g.s))
        slot_p, gate_p = _route_call(h2, w_rt, 0, g.b, g.s, g.cap_p, groups, "moe_route_ctx")
        slot_s, gate_s = _route_call(h2, w_rt, g.mp, g.bs, g.ts, g.cap_s, 1, "moe_route_lat")
        xs, gs = _gather(h2, slot_p, gate_p, slot_s, gate_s, g)
        hm = _ffn1_call(xs, moe_w_gate, moe_w_up, l)
        y = _ffn2_call(hm, moe_w_down, gs, l)
        mod_next = mod_of(l + 1) if l + 1 < depth else None
        x, h = _combine_ln(y, slot_p, slot_s, x, mod, mod_next, ln2_g[l], ln2_b[l], alpha, g)

    st_ckv, st_kpe = st_mla
    return (x.reshape(b, s, d), h.reshape(bs, ts, d), st_ckv, jnp.swapaxes(st_kpe, 2, 3),
            jnp.moveaxis(st_dk.reshape(b, n_diff, diff_heads, 2, DIFF_DK, s), 5, 2),
            st_dv.reshape(b, n_diff, s, diff_heads, DIFF_DV))
```

```python
import functools
import math

import jax
import jax.numpy as jnp
from jax import lax
from jax.experimental import pallas as pl
from jax.experimental.pallas import tpu as pltpu

F32 = jnp.float32
BF16 = jnp.bfloat16

NORM_EPS = 1e-6
ROPE_BASE = 10000.0
GRID_W = 64
CAP_FACTOR = 2
MLA_NOPE, MLA_ROPE, MLA_V = 128, 64, 128
DIFF_DK, DIFF_DV = 64, 128
ROT_DIM = 64
LANE = 128
MOD_ROWS = 8
LOG2E = math.log2(math.e)
SUBLANE = 8
Q_TILE = 256
KV_CHUNK = 256
ROW_SPLIT = 2
TILE_UNROLL = 4
CTX_GROUP = 4
V7X_VMEM_BYTES = 64 * 2**20
VMEM_HEADROOM_BYTES = 6 * 2**20


def _params(semantics, vmem_bytes):
    limit = min(int(vmem_bytes), V7X_VMEM_BYTES - VMEM_HEADROOM_BYTES)
    return pltpu.CompilerParams(dimension_semantics=semantics, vmem_limit_bytes=limit)


def _tile(n, target, mult=16):
    t = min(n, target)
    while t > mult and (n % t or t % mult):
        t -= mult
    assert n % t == 0 and t % mult == 0, (n, target, mult)
    return t


def _dot(a, b):
    return jnp.dot(a, b, preferred_element_type=F32)


def _dot_nt(a, b):
    return lax.dot_general(a, b, (((1,), (1,)), ((), ())), preferred_element_type=F32)


def _dot_tn(a, b):
    return lax.dot_general(a, b, (((0,), (0,)), ((), ())), preferred_element_type=F32)


def _sigmoid(x):
    return 1.0 / (1.0 + jnp.exp(-x))


def _rms(x):
    return x * lax.rsqrt(jnp.mean(x * x, axis=-1, keepdims=True) + NORM_EPS)


def _layer_norm(z, g, b):
    mu = jnp.mean(z, axis=-1, keepdims=True)
    zc = z - mu
    var = jnp.mean(zc * zc, axis=-1, keepdims=True)
    return zc * lax.rsqrt(var + NORM_EPS) * g + b


def _rope(x, cos, sin):
    n = x.shape[-1]
    nxt = pltpu.roll(x, n - 16, 1)
    prv = pltpu.roll(x, 16, 1)
    lane = lax.broadcasted_iota(jnp.int32, x.shape, 1)
    rot = jnp.where((lane % 32) < 16, -nxt, prv)
    return x * cos + rot * sin


def _mod_row(i, tm, mp, ts):
    r0 = i * tm
    return jnp.where(r0 < mp, 0, 1 + (r0 - mp) // ts)


def _mod_spec(which, d, row_fn):
    return pl.BlockSpec((None, 1, d), lambda *ids: (row_fn(*ids), 0, which))


def _row_vec(v):
    return v.reshape(1, -1)


def _mod_body(c_ref, w_ref, b_ref, o_ref):
    c = c_ref[...]
    s = (c * _sigmoid(c)).astype(BF16)
    o_ref[...] = _dot(s, w_ref[...].astype(BF16)) + b_ref[...]


def _mod_call(cv, ada_w, ada_b):
    nl, d, n6 = ada_w.shape
    tn = _tile(n6, 1024, LANE)
    return pl.pallas_call(
        _mod_body,
        grid=(nl, n6 // tn),
        in_specs=[pl.BlockSpec((MOD_ROWS, d), lambda l, n: (0, 0)),
                  pl.BlockSpec((None, d, tn), lambda l, n: (l, 0, n)),
                  pl.BlockSpec((None, 1, tn), lambda l, n: (l, 0, n))],
        out_specs=pl.BlockSpec((None, MOD_ROWS, tn), lambda l, n: (l, 0, n)),
        out_shape=jax.ShapeDtypeStruct((nl, MOD_ROWS, n6), F32),
        compiler_params=_params(("parallel", "parallel"), 4 * d * tn * 4 + 2**22),
        name="ada_mod",
    )(cv, ada_w, ada_b.reshape(nl, 1, n6))


def _modcast_body(xp_ref, xs_ref, sc_ref, sh_ref, x_ref, h_ref, *, n_ctx_tiles):
    def emit(src_ref):
        x = src_ref[...]
        x_ref[...] = x
        h_ref[...] = (x * (1.0 + sc_ref[...]) + sh_ref[...]).astype(BF16)

    i = pl.program_id(0)
    pl.when(i < n_ctx_tiles)(lambda: emit(xp_ref))
    pl.when(i >= n_ctx_tiles)(lambda: emit(xs_ref))


def _modcast_call(xp, xs, mod, g):
    d = xp.shape[1]
    m = g.mp + g.ms
    tm = _tile(math.gcd(g.mp, g.ts), 512)
    nct = g.mp // tm
    row = lambda i: _mod_row(i, tm, g.mp, g.ts)
    rows = lambda i: (i, 0)
    return pl.pallas_call(
        functools.partial(_modcast_body, n_ctx_tiles=nct),
        grid=(m // tm,),
        in_specs=[pl.BlockSpec((tm, d), lambda i: (jnp.minimum(i, nct - 1), 0)),
                  pl.BlockSpec((tm, d), lambda i: (jnp.maximum(i - nct, 0), 0)),
                  _mod_spec(1, d, row), _mod_spec(0, d, row)],
        out_specs=[pl.BlockSpec((tm, d), rows), pl.BlockSpec((tm, d), rows)],
        out_shape=[jax.ShapeDtypeStruct((m, d), F32), jax.ShapeDtypeStruct((m, d), BF16)],
        compiler_params=_params(("arbitrary",), 2 * tm * d * 14 + 2**22),
        name="modulate_in",
    )(xp, xs, mod, mod)


def _row_windows(tm, unit):
    n = ROW_SPLIT if tm % (ROW_SPLIT * unit) == 0 else 1
    return [slice(r, r + tm // n) for r in range(0, tm, tm // n)]


def _store_state(st_ref, win, seq, value, transposed):
    for r in range((win.stop - win.start) // seq):
        tok = slice(r * seq, (r + 1) * seq)
        st_ref[win.start // seq + r] = value[:, tok] if transposed else value[tok]


def _state_spec(tm, seq, n_ctx_tiles, layer, rows, cols, transposed, col_block):
    shape = (tm // seq, None, cols, rows) if transposed else (tm // seq, None, rows, cols)

    def index(*ids):
        i, jn = col_block(*ids)
        tile = jnp.minimum(i, n_ctx_tiles - 1)
        return (tile, layer, jn, 0) if transposed else (tile, layer, 0, jn)

    return pl.BlockSpec(shape, index)


def _mla_down_body(h_ref, w_ref, qn_ref, kvn_ref, cos_ref, sin_ref, *refs, rq, rkv, seq, n_ctx_tiles, n_alias):
    cq_ref, ckv_ref, kpe16_ref, st_ckv_ref, st_kpe_ref = refs[n_alias:]
    i = pl.program_id(0)
    wins = _row_windows(h_ref.shape[0], seq)
    accs = [_dot(h_ref[w, :], w_ref[...]) for w in wins]
    for w, a in zip(wins, accs):
        cq_ref[w, :] = (_rms(a[:, :rq]) * qn_ref[...]).astype(BF16)
        ckv = _rms(a[:, rq:rq + rkv]) * kvn_ref[...]
        ckv_ref[w, :] = ckv
        kpe = _rope(a[:, rq + rkv:], cos_ref[w, :], sin_ref[w, :])
        kpe16_ref[w, :] = kpe.astype(BF16)

        @pl.when(i < n_ctx_tiles)
        def _():
            _store_state(st_ckv_ref, w, seq, ckv, False)
            _store_state(st_kpe_ref, w, seq, kpe.T[:MLA_ROPE], True)


def _mla_down_call(h, w_cat, q_norm, kv_norm, cos_t, sin_t, layer, n_layers, states, g):
    m, d = h.shape
    rq, rkv = q_norm.shape[0], kv_norm.shape[0]
    n = w_cat.shape[1]
    tm = _tile(math.gcd(g.mp, g.ts), 512)
    assert tm % g.s == 0
    nct = g.mp // tm
    const = lambda i: (0, 0)
    rows = lambda i: (i, 0)
    n_alias = 0 if states is None else 2
    colb = lambda i: (i, 0)
    return pl.pallas_call(
        functools.partial(_mla_down_body, rq=rq, rkv=rkv, seq=g.s, n_ctx_tiles=nct, n_alias=n_alias),
        grid=(m // tm,),
        in_specs=[pl.BlockSpec((tm, d), rows),
                  pl.BlockSpec((d, n), const, pipeline_mode=pl.Buffered(1)),
                  pl.BlockSpec((1, rq), const), pl.BlockSpec((1, rkv), const),
                  pl.BlockSpec((tm, LANE), rows), pl.BlockSpec((tm, LANE), rows)]
                 + [pl.BlockSpec(memory_space=pl.ANY)] * n_alias,
        out_specs=[pl.BlockSpec((tm, rq), rows), pl.BlockSpec((tm, rkv), rows), pl.BlockSpec((tm, LANE), rows),
                   _state_spec(tm, g.s, nct, layer, g.s, rkv, False, colb),
                   _state_spec(tm, g.s, nct, layer, g.s, MLA_ROPE, True, colb)],
        out_shape=[jax.ShapeDtypeStruct((m, rq), BF16), jax.ShapeDtypeStruct((m, rkv), F32),
                   jax.ShapeDtypeStruct((m, LANE), BF16),
                   jax.ShapeDtypeStruct((g.b, n_layers, g.s, rkv), F32),
                   jax.ShapeDtypeStruct((g.b, n_layers, MLA_ROPE, g.s), F32)],
        input_output_aliases={6 + a: 3 + a for a in range(n_alias)},
        compiler_params=_params(("arbitrary",), d * n * 2 + 2 * tm * (d * 2 + n * 8) + tm * n * 8 + 2**23),
        name="mla_down",
    )(h, w_cat, _row_vec(q_norm), _row_vec(kv_norm), cos_t, sin_t, *(states or ()))


def _store_col_tiles(o_ref, win, qt, value_t):
    for r in range((win.stop - win.start) // qt):
        o_ref[win.start // qt + r] = value_t[:, r * qt:(r + 1) * qt].astype(o_ref.dtype)


def _rope_t(x, cos, sin):
    q = ROT_DIM // 4
    rot = jnp.concatenate([-x[q:2 * q], x[:q], -x[3 * q:], x[2 * q:3 * q]], axis=0)
    return x * cos + rot * sin


def _mla_q_body(cq_ref, wt_ref, cos_ref, sin_ref, q_ref, *, heads, scale):
    qt = q_ref.shape[2]
    wins = _row_windows(cq_ref.shape[0], qt)
    accs = [_dot_nt(wt_ref[...], cq_ref[w, :]) for w in wins]
    for w, at in zip(wins, accs):
        cos, sin = cos_ref[:ROT_DIM, w], sin_ref[:ROT_DIM, w]
        for h in range(heads):
            lo = 2 * LANE * h
            pe = _rope_t(at[lo + LANE:lo + LANE + ROT_DIM], cos, sin)
            blk = jnp.concatenate([at[lo:lo + LANE], pe, at[lo + LANE + ROT_DIM:lo + 2 * LANE]], axis=0) * scale
            _store_col_tiles(q_ref.at[:, lo:lo + 2 * LANE, :], w, qt, blk)


def _mla_q_call(cq, w_uq_pad_t, cos_tt, sin_tt, heads, g):
    m, rq = cq.shape
    n = w_uq_pad_t.shape[0]
    tm = _tile(math.gcd(g.mp, g.ts), 512, max(g.qt, LANE))
    const = lambda i: (0, 0)
    scale = (MLA_NOPE + MLA_ROPE) ** -0.5 * LOG2E
    return pl.pallas_call(
        functools.partial(_mla_q_body, heads=heads, scale=scale),
        grid=(m // tm,),
        in_specs=[pl.BlockSpec((tm, rq), lambda i: (i, 0)),
                  pl.BlockSpec((n, rq), const, pipeline_mode=pl.Buffered(1)),
                  pl.BlockSpec((LANE, tm), lambda i: (0, i)), pl.BlockSpec((LANE, tm), lambda i: (0, i))],
        out_specs=pl.BlockSpec((tm // g.qt, n, g.qt), lambda i: (i, 0, 0)),
        out_shape=jax.ShapeDtypeStruct((m // g.qt, n, g.qt), BF16),
        compiler_params=_params(("arbitrary",), rq * n * 2 + 2 * tm * (rq * 2 + n * 2) + 4 * tm * n * 4 + 2**23),
        name="mla_q_up",
    )(cq, w_uq_pad_t, cos_tt, sin_tt)


def _kv_up_body(x_ref, kpe_ref, w_ref, kcat_ref, vt_ref, *, heads):
    wins = _row_windows(x_ref.shape[0], LANE)
    accs = [_dot(x_ref[w, :].astype(BF16), w_ref[...]) for w in wins]
    for w, a in zip(wins, accs):
        kpe = kpe_ref[w, :]
        for h in range(heads):
            lo = 2 * LANE * h
            kcat_ref[w, lo:lo + LANE] = a[:, lo:lo + LANE].astype(BF16)
            kcat_ref[w, lo + LANE:lo + 2 * LANE] = kpe
            vt_ref[h * LANE:(h + 1) * LANE, w] = a[:, lo + LANE:lo + 2 * LANE].T.astype(BF16)


def _kv_up_call(x, kpe16, w_all, j, heads, name):
    m, k = x.shape
    n = w_all.shape[2]
    tm = _tile(m, 512)
    rows = lambda i: (i, 0)
    return pl.pallas_call(
        functools.partial(_kv_up_body, heads=heads),
        grid=(m // tm,),
        in_specs=[pl.BlockSpec((tm, k), rows), pl.BlockSpec((tm, LANE), rows),
                  pl.BlockSpec((None, k, n), lambda i: (j, 0, 0), pipeline_mode=pl.Buffered(1))],
        out_specs=[pl.BlockSpec((tm, 2 * heads * LANE), rows), pl.BlockSpec((heads * LANE, tm), lambda i: (0, i))],
        out_shape=[jax.ShapeDtypeStruct((m, 2 * heads * LANE), BF16), jax.ShapeDtypeStruct((heads * LANE, m), BF16)],
        compiler_params=_params(("arbitrary",), k * n * 2 + 2 * tm * (k * 4 + n * 3) + tm * n * 4 + 2**23),
        name=name,
    )(x, kpe16, w_all)


def _row_fold(x, op):
    out = x[:SUBLANE]
    for r in range(1, x.shape[0] // SUBLANE):
        out = op(out, x[r * SUBLANE:(r + 1) * SUBLANE])
    return out


def _chunks(segments):
    out = []
    for k_ref, vt_ref in segments:
        n = k_ref.shape[0]
        assert n % LANE == 0
        out += [(k_ref, vt_ref, r0, min(KV_CHUNK, n - r0)) for r0 in range(0, n, KV_CHUNK)]
    return out


def _scores_pass(qt, segments, s_ref):
    m = None
    off = 0
    for k_ref, _, r0, n in _chunks(segments):
        s = _dot(k_ref[r0:r0 + n, :], qt)
        s_ref[off:off + n, :] = s
        cm = _row_fold(s, jnp.maximum)
        m = cm if m is None else jnp.maximum(m, cm)
        off += n
    return jnp.max(m, axis=0, keepdims=True)


def _values_pass(segments, s_ref, m):
    acc = l = None
    off = 0
    for _, vt_ref, r0, n in _chunks(segments):
        e = jnp.exp2(s_ref[off:off + n, :] - m)
        cl = _row_fold(e, jnp.add)
        pv = _dot(vt_ref[:, r0:r0 + n], e.astype(BF16))
        l = cl if l is None else l + cl
        acc = pv if acc is None else acc + pv
        off += n
    return acc * (1.0 / jnp.sum(l, axis=0, keepdims=True))


def _pipelined_tiles(n_tiles, scores_of, values_of, s_a, s_b):
    if n_tiles == 1:
        values_of(0, s_a, scores_of(0, s_a))
        return
    unroll = TILE_UNROLL if n_tiles % TILE_UNROLL == 0 else 2
    assert n_tiles % unroll == 0 and unroll % 2 == 0
    bufs = (s_a, s_b)

    def body(j, m):
        t = unroll * j
        for u in range(unroll):
            nxt = t + u + 1 if u + 1 < unroll else jnp.minimum(t + unroll, n_tiles - 1)
            m_nxt = scores_of(nxt, bufs[(u + 1) % 2])
            values_of(t + u, bufs[u % 2], m)
            m = m_nxt
        return m

    lax.fori_loop(0, n_tiles // unroll, body, scores_of(0, s_a))


def _tile_rows(t, n):
    return pl.ds(t * n, n) if isinstance(t, int) else pl.ds(pl.multiple_of(t * n, n), n)


def _pipelined_static(tiles, scores_of, values_of, bufs):
    grp = len(bufs) // 2
    groups = [tiles[i:i + grp] for i in range(0, len(tiles), grp)]
    prev = None
    for gi, group in enumerate(groups):
        half = bufs[(gi % 2) * grp:(gi % 2 + 1) * grp]
        cur = [(tile, buf, scores_of(tile, buf)) for tile, buf in zip(group, half)]
        for tile, buf, m in prev or ():
            values_of(tile, buf, m)
        prev = cur
    for tile, buf, m in prev:
        values_of(tile, buf, m)


def _mla_attn_ctx_body(q_ref, k_ref, vt_ref, o_ref, *bufs, heads):
    nq = q_ref.shape[2]
    seg = lambda h: [(k_ref.at[:, 2 * LANE * h:2 * LANE * (h + 1)], vt_ref.at[LANE * h:LANE * (h + 1), :])]

    def scores_of(tile, s_ref):
        h, r = tile
        return _scores_pass(q_ref[r, 2 * LANE * h:2 * LANE * (h + 1), :], seg(h), s_ref)

    def values_of(tile, s_ref, m):
        h, r = tile
        o_ref[r * nq:(r + 1) * nq, LANE * h:LANE * (h + 1)] = _values_pass(seg(h), s_ref, m).T.astype(BF16)

    tiles = [(h, r) for h in range(heads) for r in range(q_ref.shape[0])]
    _pipelined_static(tiles, scores_of, values_of, bufs)


def _mla_attn_lat_body(q_ref, kc_ref, vc_ref, kn_ref, vn_ref, o_alias, o_ref, s_a, s_b):
    del o_alias
    segs = [(kc_ref, vc_ref), (kn_ref, vn_ref)]
    nq = q_ref.shape[2]

    def scores_of(t, s_ref):
        return _scores_pass(q_ref[t], segs, s_ref)

    def values_of(t, s_ref, m):
        o_ref[_tile_rows(t, nq), :] = _values_pass(segs, s_ref, m).T.astype(BF16)

    _pipelined_tiles(q_ref.shape[0], scores_of, values_of, s_a, s_b)


def _mla_attn(q, kcat, vt, kcat_c, vt_c, heads, g):
    m = kcat.shape[0]
    hn = heads * LANE
    past = kcat_c.shape[0] // g.bs
    o = pl.pallas_call(
        functools.partial(_mla_attn_ctx_body, heads=heads),
        grid=(g.b,),
        in_specs=[pl.BlockSpec((g.s // g.qt, 2 * hn, g.qt), lambda b: (b, 0, 0)),
                  pl.BlockSpec((g.s, 2 * hn), lambda b: (b, 0)),
                  pl.BlockSpec((hn, g.s), lambda b: (0, b))],
        out_specs=pl.BlockSpec((g.s, hn), lambda b: (b, 0)),
        out_shape=jax.ShapeDtypeStruct((m, hn), BF16),
        scratch_shapes=[pltpu.VMEM((g.s, g.qt), F32)] * (2 * CTX_GROUP),
        compiler_params=_params(("arbitrary",), 2 * g.s * hn * 12 + 2**24),
        name="mla_attn_ctx",
    )(q, kcat, vt)
    kb0 = g.mp // g.ts
    return pl.pallas_call(
        _mla_attn_lat_body,
        grid=(g.bs, heads),
        in_specs=[pl.BlockSpec((g.ts // g.qt, 2 * LANE, g.qt), lambda b, h: (kb0 + b, h, 0)),
                  pl.BlockSpec((past, 2 * LANE), lambda b, h: (b, h)),
                  pl.BlockSpec((LANE, past), lambda b, h: (h, b)),
                  pl.BlockSpec((g.ts, 2 * LANE), lambda b, h: (kb0 + b, h)),
                  pl.BlockSpec((LANE, g.ts), lambda b, h: (h, kb0 + b)),
                  pl.BlockSpec(memory_space=pl.ANY)],
        out_specs=pl.BlockSpec((g.ts, LANE), lambda b, h: (kb0 + b, h)),
        out_shape=jax.ShapeDtypeStruct((m, hn), BF16),
        input_output_aliases={5: 0},
        scratch_shapes=[pltpu.VMEM((past + g.ts, g.qt), F32)] * 2,
        compiler_params=_params(("arbitrary", "arbitrary"), 2**25),
        name="mla_attn_lat",
    )(q, kcat_c, vt_c, kcat, vt, o)


def _diff_lambda(lam_ref, lam_init):
    lv = lam_ref[...]
    s1 = jnp.sum(lv[0:1] * lv[1:2], axis=-1, keepdims=True)
    s2 = jnp.sum(lv[2:3] * lv[3:4], axis=-1, keepdims=True)
    return jnp.exp(s1) - jnp.exp(s2) + lam_init


def _diff_queries_t(qt):
    qf = qt.astype(F32)
    row = lax.broadcasted_iota(jnp.int32, qf.shape, 0)
    return jnp.concatenate([jnp.where(row < DIFF_DK, qf, 0.0), jnp.where(row < DIFF_DK, 0.0, qf)],
                           axis=1).astype(BF16)


def _diff_combine(ot, lam, subln_col, lam_init):
    n = ot.shape[1] // 2
    dt = ot[:, :n] - lam * ot[:, n:]
    dt = dt * lax.rsqrt(jnp.mean(dt * dt, axis=0, keepdims=True) + NORM_EPS) * subln_col * (1.0 - lam_init)
    return dt.T.astype(BF16)


def _diff_attn_ctx_body(q_ref, k_ref, vt_ref, lam_ref, sub_ref, o_ref, *bufs, heads, lam_init):
    lam = _diff_lambda(lam_ref, lam_init)
    nq = q_ref.shape[2]
    cols = lambda h: slice(LANE * h, LANE * (h + 1))
    seg = lambda h: [(k_ref.at[:, cols(h)], vt_ref.at[cols(h), :])]

    def scores_of(tile, s_ref):
        h, r = tile
        return _scores_pass(_diff_queries_t(q_ref[r, cols(h), :]), seg(h), s_ref)

    def values_of(tile, s_ref, m):
        h, r = tile
        o_ref[r * nq:(r + 1) * nq, cols(h)] = _diff_combine(_values_pass(seg(h), s_ref, m), lam, sub_ref[...],
                                                           lam_init)

    tiles = [(h, r) for h in range(heads) for r in range(q_ref.shape[0])]
    _pipelined_static(tiles, scores_of, values_of, bufs)


def _diff_attn_lat_body(q_ref, kc_ref, vc_ref, kn_ref, vn_ref, lam_ref, sub_ref, o_alias, o_ref, s_a, s_b, *,
                        lam_init):
    del o_alias
    lam = _diff_lambda(lam_ref, lam_init)
    segs = [(kc_ref, vc_ref), (kn_ref, vn_ref)]
    nq = q_ref.shape[2]

    def scores_of(t, s_ref):
        return _scores_pass(_diff_queries_t(q_ref[t]), segs, s_ref)

    def values_of(t, s_ref, m):
        o_ref[_tile_rows(t, nq), :] = _diff_combine(_values_pass(segs, s_ref, m), lam, sub_ref[...], lam_init)

    _pipelined_tiles(q_ref.shape[0], scores_of, values_of, s_a, s_b)


def _diff_attn(q, k, vt, k_c, vt_c, lam_vecs, subln, lam_init, heads, g):
    m = k.shape[0]
    hn = heads * LANE
    past = k_c.shape[0] // g.bs
    sub = subln.reshape(-1, 1)
    blk = lambda b: (b, 0)
    qd = g.qt // 2
    o = pl.pallas_call(
        functools.partial(_diff_attn_ctx_body, heads=heads, lam_init=lam_init),
        grid=(g.b,),
        in_specs=[pl.BlockSpec((g.s // qd, hn, qd), lambda b: (b, 0, 0)), pl.BlockSpec((g.s, hn), blk),
                  pl.BlockSpec((hn, g.s), lambda b: (0, b)),
                  pl.BlockSpec((4, DIFF_DK), lambda b: (0, 0)), pl.BlockSpec((DIFF_DV, 1), lambda b: (0, 0))],
        out_specs=pl.BlockSpec((g.s, hn), blk),
        out_shape=jax.ShapeDtypeStruct((m, hn), BF16),
        scratch_shapes=[pltpu.VMEM((g.s, g.qt), F32)] * (2 * CTX_GROUP),
        compiler_params=_params(("arbitrary",), 2 * g.s * hn * 12 + 2**24),
        name="diff_attn_ctx",
    )(q, k, vt, lam_vecs, sub)
    kb0 = g.mp // g.ts
    return pl.pallas_call(
        functools.partial(_diff_attn_lat_body, lam_init=lam_init),
        grid=(g.bs, heads),
        in_specs=[pl.BlockSpec((g.ts // qd, LANE, qd), lambda b, h: (kb0 + b, h, 0)),
                  pl.BlockSpec((past, LANE), lambda b, h: (b, h)),
                  pl.BlockSpec((LANE, past), lambda b, h: (h, b)),
                  pl.BlockSpec((g.ts, LANE), lambda b, h: (kb0 + b, h)),
                  pl.BlockSpec((LANE, g.ts), lambda b, h: (h, kb0 + b)),
                  pl.BlockSpec((4, DIFF_DK), lambda b, h: (0, 0)),
                  pl.BlockSpec((DIFF_DV, 1), lambda b, h: (0, 0)),
                  pl.BlockSpec(memory_space=pl.ANY)],
        out_specs=pl.BlockSpec((g.ts, LANE), lambda b, h: (kb0 + b, h)),
        out_shape=jax.ShapeDtypeStruct((m, hn), BF16),
        input_output_aliases={7: 0},
        scratch_shapes=[pltpu.VMEM((past + g.ts, g.qt), F32)] * 2,
        compiler_params=_params(("arbitrary", "arbitrary"), 2**25),
        name="diff_attn_lat",
    )(q, k_c, vt_c, k, vt, lam_vecs, sub, o)


def _proj_body(h_ref, wt_ref, cos_ref, sin_ref, *refs, rope, scale, kinds, seq, n_ctx_tiles, n_alias):
    out_refs = refs[n_alias:]
    i = pl.program_id(1)
    wins = _row_windows(h_ref.shape[0], max(seq, LANE))
    accs = [_dot_nt(wt_ref[...], h_ref[w, :]) for w in wins]
    for w, at in zip(wins, accs):
        if rope:
            cos, sin = cos_ref[:ROT_DIM, w], sin_ref[:ROT_DIM, w]
            at = jnp.concatenate([_rope_t(at[r:r + ROT_DIM], cos, sin) for r in range(0, at.shape[0], ROT_DIM)],
                                 axis=0)
        if scale != 1.0:
            at = at * scale
        for o_ref, kind in zip(out_refs, kinds):
            if kind == "rows":
                o_ref[w, :] = at.T.astype(o_ref.dtype)
            elif kind == "cols":
                o_ref[:, w] = at.astype(o_ref.dtype)
            elif kind == "col_tiles":
                _store_col_tiles(o_ref, w, o_ref.shape[2], at)
            else:
                @pl.when(i < n_ctx_tiles)
                def _(o_ref=o_ref, kind=kind, at=at, w=w):
                    _store_state(o_ref, w, seq, at if kind == "state_cols" else at.T, kind == "state_cols")


def _proj_call(h, wt_all, layer, col0, ncols, cos_tt, sin_tt, rope, scale, outs, st_layer, st_layers, state, g, name):
    m, d = h.shape
    tn = _tile(ncols, 1024, LANE)
    tm = _tile(math.gcd(g.mp, g.ts), 512, LANE)
    assert tm % g.s == 0
    nct = g.mp // tm
    nb0 = col0 // tn
    colb = lambda j, i: (i, j)
    specs, shapes = [], []
    for kind, dt in outs:
        if kind == "rows":
            specs.append(pl.BlockSpec((tm, tn), lambda j, i: (i, j)))
            shapes.append(jax.ShapeDtypeStruct((m, ncols), dt))
        elif kind == "cols":
            specs.append(pl.BlockSpec((tn, tm), lambda j, i: (j, i)))
            shapes.append(jax.ShapeDtypeStruct((ncols, m), dt))
        elif kind == "col_tiles":
            qd = g.qt // 2
            specs.append(pl.BlockSpec((tm // qd, tn, qd), lambda j, i: (i, j, 0)))
            shapes.append(jax.ShapeDtypeStruct((m // qd, ncols, qd), dt))
        else:
            tr = kind == "state_cols"
            specs.append(_state_spec(tm, g.s, nct, st_layer, g.s, tn, tr, colb))
            shapes.append(jax.ShapeDtypeStruct((g.b, st_layers, ncols, g.s) if tr else (g.b, st_layers, g.s, ncols), dt))
    st_idx = [k for k, (kind, _) in enumerate(outs) if kind.startswith("state")]
    n_alias = len(st_idx) if state is not None else 0
    return pl.pallas_call(
        functools.partial(_proj_body, rope=rope, scale=scale, kinds=tuple(k for k, _ in outs), seq=g.s,
                          n_ctx_tiles=nct, n_alias=n_alias),
        grid=(ncols // tn, m // tm),
        in_specs=[pl.BlockSpec((tm, d), lambda j, i: (i, 0)),
                  pl.BlockSpec((None, tn, d), lambda j, i: (layer, nb0 + j, 0)),
                  pl.BlockSpec((LANE, tm), lambda j, i: (0, i)), pl.BlockSpec((LANE, tm), lambda j, i: (0, i))]
                 + [pl.BlockSpec(memory_space=pl.ANY)] * n_alias,
        out_specs=specs,
        out_shape=shapes,
        input_output_aliases={4 + a: st_idx[a] for a in range(n_alias)},
        compiler_params=_params(("arbitrary", "arbitrary"),
                                d * tn * 4 + 2 * tm * (d * 2 + tn * 6) + 4 * tm * tn * 4 + 2**23),
        name=name,
    )(h, wt_all, cos_tt, sin_tt, *([state] if n_alias else []))


def _out_ln_body(o_ref, w_ref, x_ref, gate_ref, g_ref, b_ref, sc_ref, sh_ref, xo_ref, ho_ref, *, alpha):
    wins = _row_windows(o_ref.shape[0], 16)
    accs = [_dot(o_ref[w, :], w_ref[...]) for w in wins]
    for w, y in zip(wins, accs):
        xn = _layer_norm(alpha * x_ref[w, :] + gate_ref[...] * y, g_ref[...], b_ref[...])
        xo_ref[w, :] = xn
        ho_ref[w, :] = (xn * (1.0 + sc_ref[...]) + sh_ref[...]).astype(BF16)


def _out_ln_call(o, w_o_all, layer, x, mod, ln_g, ln_b, alpha, g):
    m, k = o.shape
    d = w_o_all.shape[2]
    tm = _tile(math.gcd(g.mp, g.ts), 512)
    row = lambda i: _mod_row(i, tm, g.mp, g.ts)
    const = lambda i: (0, 0)
    rows = lambda i: (i, 0)
    return pl.pallas_call(
        functools.partial(_out_ln_body, alpha=alpha),
        grid=(m // tm,),
        in_specs=[pl.BlockSpec((tm, k), rows),
                  pl.BlockSpec((None, k, d), lambda i: (layer, 0, 0), pipeline_mode=pl.Buffered(1)),
                  pl.BlockSpec((tm, d), rows), _mod_spec(2, d, row),
                  pl.BlockSpec((1, d), const), pl.BlockSpec((1, d), const),
                  _mod_spec(4, d, row), _mod_spec(3, d, row)],
        out_specs=[pl.BlockSpec((tm, d), rows), pl.BlockSpec((tm, d), rows)],
        out_shape=[jax.ShapeDtypeStruct((m, d), F32), jax.ShapeDtypeStruct((m, d), BF16)],
        compiler_params=_params(("arbitrary",), k * d * 2 + 2 * tm * (k * 2 + d * 10) + 4 * tm * d * 4 + 2**23),
        name="out_proj_ln",
    )(o, w_o_all, x, mod, _row_vec(ln_g), _row_vec(ln_b), mod, mod)


def _lane_cumsum(x, tri):
    c = tri.shape[0]
    carry = jnp.zeros((x.shape[0], 1), F32)
    parts = []
    for j in range(x.shape[1] // c):
        xc = x[:, j * c:(j + 1) * c]
        parts.append(_dot(xc.astype(BF16), tri) + carry)
        carry = carry + jnp.sum(xc, axis=1, keepdims=True)
    return parts[0] if len(parts) == 1 else jnp.concatenate(parts, axis=1)


def _route_body(h_ref, wr_ref, slot_ref, gate_ref, *, groups, t, cap):
    logits = _dot_nt(wr_ref[...].astype(BF16), h_ref[...])
    ex = jnp.exp(logits - jnp.max(logits, axis=0, keepdims=True))
    aff = ex / jnp.sum(ex, axis=0, keepdims=True)
    nbits = jnp.finfo(aff.dtype).bits
    int_t = jnp.dtype(f"int{nbits}")
    as_float = lambda bits: lax.bitcast_convert_type(bits, aff.dtype)
    c = min(t, 2 * LANE)
    tri = jnp.where(lax.broadcasted_iota(jnp.int32, (c, c), 0) <= lax.broadcasted_iota(jnp.int32, (c, c), 1),
                    1.0, 0.0).astype(BF16)
    capf = float(cap)
    for gi in range(groups):
        k = aff[:, gi * t:(gi + 1) * t]
        thr = jnp.zeros((k.shape[0], 1), int_t)
        for bit in range(nbits - 2, -1, -1):
            cand = thr | (1 << bit)
            cnt = jnp.sum(jnp.where(k >= as_float(cand), 1.0, 0.0), axis=1, keepdims=True)
            thr = jnp.where(cnt >= capf, cand, thr)
        thr = as_float(thr)
        gt = jnp.where(k > thr, 1.0, 0.0)
        eq = jnp.where(k == thr, 1.0, 0.0)
        need = capf - jnp.sum(gt, axis=1, keepdims=True)
        eq_before = _lane_cumsum(eq, tri) - eq
        sel = gt + eq * jnp.where(eq_before < need, 1.0, 0.0)
        pos = _lane_cumsum(sel, tri) - 1.0
        slot_ref[gi] = jnp.where(sel > 0.5, pos, -1.0).astype(jnp.int32)
        gate_ref[gi] = aff[:, gi * t:(gi + 1) * t]


def _route_call(h2, w_router_t, row0, nreq, t, cap, groups, name):
    e, d = w_router_t.shape
    rb0 = row0 // (groups * t)
    return pl.pallas_call(
        functools.partial(_route_body, groups=groups, t=t, cap=cap),
        grid=(nreq // groups,),
        in_specs=[pl.BlockSpec((groups * t, d), lambda i: (rb0 + i, 0)), pl.BlockSpec((e, d), lambda i: (0, 0))],
        out_specs=[pl.BlockSpec((groups, e, t), lambda i: (i, 0, 0)), pl.BlockSpec((groups, e, t), lambda i: (i, 0, 0))],
        out_shape=[jax.ShapeDtypeStruct((nreq, e, t), jnp.int32), jax.ShapeDtypeStruct((nreq, e, t), F32)],
        compiler_params=_params(("parallel",), 2 * groups * t * d * 2 + 2**24),
        name=name,
    )(h2, w_router_t)


def _one_hot_rows(slot_row, cap):
    r = lax.broadcasted_iota(jnp.int32, (cap, slot_row.shape[1]), 0)
    return slot_row == r


def _gather_ctx_body(h_ref, slot_ref, gate_ref, xs_ref, gs_ref, *, cap):
    nreq, ne, s = slot_ref.shape
    for r in range(nreq):
        masks = [_one_hot_rows(slot_ref[r, e:e + 1, :], cap) for e in range(ne)]
        p = jnp.concatenate([jnp.where(mk, 1.0, 0.0) for mk in masks], axis=0).astype(BF16)
        xs = _dot(p, h_ref[r * s:(r + 1) * s, :]).astype(BF16)
        for e in range(ne):
            xs_ref[e, r * cap:(r + 1) * cap] = xs[e * cap:(e + 1) * cap]
            gs_ref[e, r * cap:(r + 1) * cap] = jnp.sum(jnp.where(masks[e], gate_ref[r, e:e + 1, :], 0.0),
                                                       axis=1, keepdims=True)


def _gather_lat_body(h_ref, slot_ref, gate_ref, xs_alias, gs_alias, xs_ref, gs_ref, *, cap):
    del xs_alias, gs_alias
    e = pl.program_id(1)
    mk = _one_hot_rows(slot_ref[pl.ds(e, 1), :], cap)
    xs_ref[...] = _dot(jnp.where(mk, 1.0, 0.0).astype(BF16), h_ref[...]).astype(BF16)
    gs_ref[...] = jnp.sum(jnp.where(mk, gate_ref[pl.ds(e, 1), :], 0.0), axis=1, keepdims=True)


def _gather(h2, slot_p, gate_p, slot_s, gate_s, g):
    d = h2.shape[1]
    ne = slot_p.shape[1]
    rows_e = g.b * g.cap_p + g.bs * g.cap_s
    shapes = [jax.ShapeDtypeStruct((ne, rows_e, d), BF16), jax.ShapeDtypeStruct((ne, rows_e, 1), F32)]
    nr = g.ctx_reqs
    xs, gs = pl.pallas_call(
        functools.partial(_gather_ctx_body, cap=g.cap_p),
        grid=(g.b // nr,),
        in_specs=[pl.BlockSpec((nr * g.s, d), lambda b: (b, 0)),
                  pl.BlockSpec((nr, ne, g.s), lambda b: (b, 0, 0)), pl.BlockSpec((nr, ne, g.s), lambda b: (b, 0, 0))],
        out_specs=[pl.BlockSpec((ne, nr * g.cap_p, d), lambda b: (0, b, 0)),
                   pl.BlockSpec((ne, nr * g.cap_p, 1), lambda b: (0, b, 0))],
        out_shape=shapes,
        compiler_params=_params(("parallel",), 2 * nr * ne * g.cap_p * (d * 8 + g.s * 8) + 2**24),
        name="moe_gather_ctx",
    )(h2, slot_p, gate_p)
    blk0 = (g.b * g.cap_p) // g.cap_s
    hb0 = g.mp // g.ts
    return pl.pallas_call(
        functools.partial(_gather_lat_body, cap=g.cap_s),
        grid=(g.bs, ne),
        in_specs=[pl.BlockSpec((g.ts, d), lambda b, e: (hb0 + b, 0)),
                  pl.BlockSpec((None, ne, g.ts), lambda b, e: (b, 0, 0)), pl.BlockSpec((None, ne, g.ts), lambda b, e: (b, 0, 0)),
                  pl.BlockSpec(memory_space=pl.ANY), pl.BlockSpec(memory_space=pl.ANY)],
        out_specs=[pl.BlockSpec((None, g.cap_s, d), lambda b, e: (e, blk0 + b, 0)),
                   pl.BlockSpec((None, g.cap_s, 1), lambda b, e: (e, blk0 + b, 0))],
        out_shape=shapes,
        input_output_aliases={3: 0, 4: 1},
        compiler_params=_params(("parallel", "parallel"), 2 * g.ts * d * 2 + g.cap_s * (g.ts * 12 + d * 12) + 2**24),
        name="moe_gather_lat",
    )(h2, slot_s, gate_s, xs, gs)


def _ffn1_body(x_ref, wg_ref, wu_ref, o_ref):
    x = x_ref[...]
    a = _dot(x, wg_ref[...].astype(BF16))
    u = _dot(x, wu_ref[...].astype(BF16))
    o_ref[...] = (a * _sigmoid(a) * u).astype(BF16)


def _ffn1_call(xs, w_gate, w_up, layer):
    ne, rows, d = xs.shape
    f = w_gate.shape[3]
    tn = _tile(f, 512, LANE)
    wspec = pl.BlockSpec((None, None, d, tn), lambda e, n: (layer, e, 0, n))
    return pl.pallas_call(
        _ffn1_body,
        grid=(ne, f // tn),
        in_specs=[pl.BlockSpec((None, rows, d), lambda e, n: (e, 0, 0)), wspec, wspec],
        out_specs=pl.BlockSpec((None, rows, tn), lambda e, n: (e, 0, n)),
        out_shape=jax.ShapeDtypeStruct((ne, rows, f), BF16),
        compiler_params=_params(("parallel", "parallel"), 2 * rows * d * 2 + 4 * d * tn * 4 + 2 * d * tn * 2 + 5 * rows * tn * 4 + 2**23),
        name="moe_ffn_gate_up",
    )(xs, w_gate, w_up)


def _ffn2_body(h_ref, w_ref, gs_ref, o_ref):
    o_ref[...] = (_dot(h_ref[...], w_ref[...].astype(BF16)) * gs_ref[...]).astype(BF16)


def _ffn2_call(hm, w_down, gs, layer):
    ne, rows, f = hm.shape
    d = w_down.shape[3]
    tn = _tile(d, 512, LANE)
    return pl.pallas_call(
        _ffn2_body,
        grid=(ne, d // tn),
        in_specs=[pl.BlockSpec((None, rows, f), lambda e, n: (e, 0, 0)),
                  pl.BlockSpec((None, None, f, tn), lambda e, n: (layer, e, 0, n)),
                  pl.BlockSpec((None, rows, 1), lambda e, n: (e, 0, 0))],
        out_specs=pl.BlockSpec((None, rows, tn), lambda e, n: (e, 0, n)),
        out_shape=jax.ShapeDtypeStruct((ne, rows, d), BF16),
        compiler_params=_params(("parallel", "parallel"), 2 * rows * f * 2 + 2 * f * tn * 4 + f * tn * 2 + 3 * rows * tn * 4 + 2 * rows * LANE * 4 + 2**23),
        name="moe_ffn_down",
    )(hm, w_down, gs)


def _combine(y, slot, cap):
    ne = y.shape[0]
    p = jnp.concatenate([jnp.where(_one_hot_rows(slot[e:e + 1, :], cap), 1.0, 0.0) for e in range(ne)], axis=0)
    return _dot_tn(p.astype(BF16), y.reshape(ne * cap, y.shape[2]))


def _combine_ln_body(*refs, cap, alpha, n_alias, emit_h, nreq):
    y_ref, slot_ref, x_ref, gate_ref, g_ref, b_ref = refs[:6]
    rest = refs[6:]
    if emit_h:
        sc_ref, sh_ref = rest[:2]
        rest = rest[2:]
    outs = rest[n_alias:]
    s = x_ref.shape[0] // max(nreq, 1)
    for r in range(max(nreq, 1)):
        rows = slice(r * s, (r + 1) * s)
        slot = slot_ref[r] if nreq else slot_ref[...]
        moe = _combine(y_ref[:, r * cap:(r + 1) * cap, :], slot, cap)
        xn = _layer_norm(alpha * x_ref[rows, :] + gate_ref[...] * moe, g_ref[...], b_ref[...])
        outs[0][rows, :] = xn
        if emit_h:
            outs[1][rows, :] = (xn * (1.0 + sc_ref[...]) + sh_ref[...]).astype(BF16)


def _combine_ln(y, slot_p, slot_s, x, mod, mod_next, ln_g, ln_b, alpha, g):
    ne, rows_e, d = y.shape
    m = x.shape[0]
    emit_h = mod_next is not None
    lg, lb = _row_vec(ln_g), _row_vec(ln_b)
    shapes = [jax.ShapeDtypeStruct((m, d), F32), jax.ShapeDtypeStruct((m, d), BF16)] if emit_h else \
        [jax.ShapeDtypeStruct((g.mp, d), F32)]
    n_out = len(shapes)

    def specs(row):
        sp = [pl.BlockSpec((1, d), lambda *a: (0, 0)), pl.BlockSpec((1, d), lambda *a: (0, 0))]
        md = [_mod_spec(5, d, row)]
        nx = [_mod_spec(1, d, row), _mod_spec(0, d, row)] if emit_h else []
        return md, sp, nx

    nr = g.ctx_reqs
    md, sp, nx = specs(lambda b: 0)
    outs = pl.pallas_call(
        functools.partial(_combine_ln_body, cap=g.cap_p, alpha=alpha, n_alias=0, emit_h=emit_h, nreq=nr),
        grid=(g.b // nr,),
        in_specs=[pl.BlockSpec((ne, nr * g.cap_p, d), lambda b: (0, b, 0)),
                  pl.BlockSpec((nr, ne, g.s), lambda b: (b, 0, 0)),
                  pl.BlockSpec((nr * g.s, d), lambda b: (b, 0))] + md + sp + nx,
        out_specs=[pl.BlockSpec((nr * g.s, d), lambda b: (b, 0))] * n_out,
        out_shape=shapes,
        compiler_params=_params(("parallel",), 2 * nr * (ne * g.cap_p * (d * 2 + g.s * 8) + g.s * d * 40) + 2**24),
        name="moe_combine_ln_ctx",
    )(y, slot_p, x, mod, lg, lb, *([mod_next, mod_next] if emit_h else []))
    outs = list(outs) if isinstance(outs, (list, tuple)) else [outs]
    tt = _tile(g.ts, 256, LANE)
    nt = g.ts // tt
    blk0 = (g.b * g.cap_p) // g.cap_s
    xb0 = g.mp // tt
    md, sp, nx = specs(lambda b, i: 1 + b)
    n_in = 6 + len(nx)
    n_alias = n_out if emit_h else 0
    ob0 = xb0 if emit_h else 0
    outs_s = pl.pallas_call(
        functools.partial(_combine_ln_body, cap=g.cap_s, alpha=alpha, n_alias=n_alias, emit_h=emit_h, nreq=0),
        grid=(g.bs, nt),
        in_specs=[pl.BlockSpec((ne, g.cap_s, d), lambda b, i: (0, blk0 + b, 0), pipeline_mode=pl.Buffered(1)),
                  pl.BlockSpec((None, ne, tt), lambda b, i: (b, 0, i)),
                  pl.BlockSpec((tt, d), lambda b, i: (xb0 + b * nt + i, 0))] + md + sp + nx
                 + [pl.BlockSpec(memory_space=pl.ANY)] * n_alias,
        out_specs=[pl.BlockSpec((tt, d), lambda b, i: (ob0 + b * nt + i, 0))] * n_out,
        out_shape=shapes if emit_h else [jax.ShapeDtypeStruct((g.ms, d), F32)],
        input_output_aliases={n_in + j: j for j in range(n_alias)},
        compiler_params=_params(("parallel", "parallel"), ne * g.cap_s * (d * 2 + tt * 10) + tt * d * 48 + 2**24),
        name="moe_combine_ln_lat",
    )(y, slot_s, x, mod, lg, lb, *([mod_next, mod_next] if emit_h else []), *(outs if emit_h else []))
    outs_s = list(outs_s) if isinstance(outs_s, (list, tuple)) else [outs_s]
    if emit_h:
        return outs_s[0], outs_s[1]
    return outs[0], outs_s[0]


class _Geom:
    def __init__(self, b, s, bs, ts, ne):
        self.b, self.s, self.bs, self.ts = b, s, bs, ts
        self.mp, self.ms = b * s, bs * ts
        self.cap_p = CAP_FACTOR * s // ne
        self.cap_s = CAP_FACTOR * ts // ne
        self.qt = min(Q_TILE, s)
        self.ctx_reqs = 2 if b % 2 == 0 else 1
        assert ts % s == 0 and self.mp % ts == 0 and bs + 1 <= MOD_ROWS
        assert (b * self.cap_p) % self.cap_s == 0 and self.cap_p % 16 == 0


def _rope_tables(g):
    rows = g.ts // GRID_W
    row = jnp.repeat(jnp.arange(rows, dtype=F32), GRID_W)
    col = jnp.tile(jnp.arange(GRID_W, dtype=F32), rows)
    half = ROT_DIM // 2
    inv_freq = ROPE_BASE ** (-jnp.arange(0, half, 2, dtype=F32) / half)
    ang_r = row[:, None] * inv_freq
    ang_c = col[:, None] * inv_freq
    ang = jnp.concatenate([ang_r, ang_r, ang_c, ang_c] * (LANE // ROT_DIM), axis=-1)
    cos = jnp.concatenate([jnp.ones((g.mp, LANE), F32), jnp.tile(jnp.cos(ang), (g.bs, 1))], axis=0)
    sin = jnp.concatenate([jnp.zeros((g.mp, LANE), F32), jnp.tile(jnp.sin(ang), (g.bs, 1))], axis=0)
    return cos, sin


def _pad_heads(w, heads, width):
    k = w.shape[0]
    per = w.shape[1] // heads
    return jnp.pad(w.reshape(k, heads, per), ((0, 0), (0, 0), (0, width - per))).reshape(k, heads * width)


def kernel(x_prompt, x_sample, cache_mla_ckv, cache_mla_kpe, cache_diff_k, cache_diff_v, c, c_ctx, ada_w, ada_b, ln1_g, ln1_b, ln2_g, ln2_b, mla_w_dq, mla_q_norm, mla_w_uq, mla_w_dkv, mla_kv_norm, mla_w_ukv, mla_w_o, diff_w_qkv, diff_lambda_q1, diff_lambda_k1, diff_lambda_q2, diff_lambda_k2, diff_subln, diff_w_o, moe_w_router, moe_w_gate, moe_w_up, moe_w_down):
    b, s, d = x_prompt.shape
    bs, ts, _ = x_sample.shape
    depth = ada_w.shape[0]
    ne = moe_w_router.shape[-1]
    g = _Geom(b, s, bs, ts, ne)
    mla_heads = mla_w_o.shape[1] // MLA_V
    diff_heads = diff_w_o.shape[1] // DIFF_DV
    alpha = (2.0 * depth) ** 0.25
    past = cache_mla_ckv.shape[2]

    cv = jnp.concatenate([c_ctx[None, :], c, jnp.zeros((MOD_ROWS - 1 - bs, d), F32)], axis=0)
    mods = _mod_call(cv, ada_w, ada_b)
    mod_of = lambda l: mods[l].reshape(MOD_ROWS, 1, 6 * d)
    cos_t, sin_t = _rope_tables(g)
    x, h = _modcast_call(x_prompt.reshape(g.mp, d), x_sample.reshape(g.ms, d), mod_of(0), g)

    n_mla, n_diff = mla_w_o.shape[0], diff_w_o.shape[0]
    mla_w_ukv, mla_w_o, diff_w_o = (w.astype(BF16) for w in (mla_w_ukv, mla_w_o, diff_w_o))
    diff_wt_qkv = jnp.swapaxes(diff_w_qkv, 1, 2).astype(BF16)
    cos_tt, sin_tt = cos_t.T, sin_t.T
    st_mla = st_dk = st_dv = None
    for l in range(depth):
        mod = mod_of(l)
        j = l // 2
        if l % 2 == 0:
            w_down = jnp.pad(jnp.concatenate([mla_w_dq[j], mla_w_dkv[j]], axis=1),
                             ((0, 0), (0, LANE - MLA_ROPE))).astype(BF16)
            cq, ckv, kpe16, *st_mla = _mla_down_call(h, w_down, mla_q_norm[j], mla_kv_norm[j], cos_t, sin_t,
                                                     j, n_mla, st_mla, g)
            w_uq_t = _pad_heads(mla_w_uq[j], mla_heads, 2 * LANE).T.astype(BF16)
            q = _mla_q_call(cq, w_uq_t, cos_tt, sin_tt, mla_heads, g)
            kpe_c = jnp.pad(cache_mla_kpe[:, j], ((0, 0), (0, 0), (0, LANE - MLA_ROPE))).astype(BF16)
            kcat, vt = _kv_up_call(ckv, kpe16, mla_w_ukv, j, mla_heads, "mla_kv_up")
            kcat_c, vt_c = _kv_up_call(cache_mla_ckv[:, j].reshape(bs * past, -1), kpe_c.reshape(bs * past, LANE),
                                       mla_w_ukv, j, mla_heads, "mla_kv_up_cache")
            o = _mla_attn(q, kcat, vt, kcat_c, vt_c, mla_heads, g)
            w_o = mla_w_o
        else:
            lam_init = 0.8 - 0.6 * math.exp(-0.3 * l)
            nq = diff_heads * 2 * DIFF_DK
            nv = diff_heads * DIFF_DV
            q, = _proj_call(h, diff_wt_qkv, j, 0, nq, cos_tt, sin_tt, True, DIFF_DK ** -0.5 * LOG2E,
                            [("col_tiles", BF16)], j, n_diff, None, g, "diff_q")
            k16, st_dk = _proj_call(h, diff_wt_qkv, j, nq, nq, cos_tt, sin_tt, True, 1.0,
                                    [("rows", BF16), ("state_cols", F32)], j, n_diff, st_dk, g, "diff_k")
            vt16, st_dv = _proj_call(h, diff_wt_qkv, j, 2 * nq, nv, cos_tt, sin_tt, False, 1.0,
                                     [("cols", BF16), ("state_rows", F32)], j, n_diff, st_dv, g, "diff_v")
            k_c = cache_diff_k[:, j].reshape(bs * past, nq).astype(BF16)
            vt_c = cache_diff_v[:, j].reshape(bs * past, nv).T.astype(BF16)
            lam_vecs = jnp.stack([diff_lambda_q1[j], diff_lambda_k1[j], diff_lambda_q2[j], diff_lambda_k2[j]])
            o = _diff_attn(q, k16, vt16, k_c, vt_c, lam_vecs, diff_subln[j], lam_init, diff_heads, g)
            w_o = diff_w_o
        x, h2 = _out_ln_call(o, w_o, j, x, mod, ln1_g[l], ln1_b[l], alpha, g)

        w_rt = moe_w_router[l].T
        groups = max(1, min(g.b, g.ts // g.s))
        slot_p, gate_p = _route_call(h2, w_rt, 0, g.b, g.s, g.cap_p, groups, "moe_route_ctx")
        slot_s, gate_s = _route_call(h2, w_rt, g.mp, g.bs, g.ts, g.cap_s, 1, "moe_route_lat")
        xs, gs = _gather(h2, slot_p, gate_p, slot_s, gate_s, g)
        hm = _ffn1_call(xs, moe_w_gate, moe_w_up, l)
        y = _ffn2_call(hm, moe_w_down, gs, l)
        mod_next = mod_of(l + 1) if l + 1 < depth else None
        x, h = _combine_ln(y, slot_p, slot_s, x, mod, mod_next, ln2_g[l], ln2_b[l], alpha, g)

    st_ckv, st_kpe = st_mla
    return (x.reshape(b, s, d), h.reshape(bs, ts, d), st_ckv, jnp.swapaxes(st_kpe, 2, 3),
            jnp.moveaxis(st_dk.reshape(b, n_diff, diff_heads, 2, DIFF_DK, s), 5, 2),
            st_dv.reshape(b, n_diff, s, diff_heads, DIFF_DV))
```
